```python
import jax, jax.numpy as jnp
from jax import lax
import numpy as np

D_MODEL = 2048
BATCH = 4
SEQ = 2048
DEPTH = 2
DEC_BATCH = 128
DEC_SEQ = 8
PAST_LEN = 16384
PAGE_SIZE = 128

G_A = 8
DG_A = D_MODEL // 16
D_A = G_A * DG_A
CHUNK = 128
H_B = 16
DK = D_MODEL // H_B
DV = D_MODEL // H_B
D_BK = H_B * DK
D_BV = H_B * DV
HG_CHUNK = 16
D_FF = 5632
CONV_W = 3
N_IN = 2 * D_A + 2 * D_BK + 2 * D_BV + 2 * D_MODEL
SPLITS = tuple(int(s) for s in np.cumsum([D_A, D_A, D_BK, D_BK, D_BV, D_BV, D_MODEL]))
EPS = 1e-6

kernel_name = "hybrid_gmlp_hgrn2_convffn_adaln_step"


def _rmsnorm(x, g):
    xf = x.astype(jnp.float32)
    y = xf * lax.rsqrt(jnp.mean(xf * xf, axis=-1, keepdims=True) + EPS) * g.astype(jnp.float32)
    return y.astype(x.dtype)


def _spatial_gate(u, v, w_s, b_s):
    B, L, G, DG = v.shape
    Lc = min(L, CHUNK)
    n = L // Lc
    w = jnp.tril(w_s[:, :Lc, :Lc])
    vc = v.reshape(B, n, Lc, G, DG)
    s = jnp.einsum('gts,bnsgc->bntgc', w, vc) + b_s[:, :Lc].T[None, None, :, :, None]
    return u * s.reshape(B, L, G, DG)


def _hgrn2(q, k, v, log_f, s0):
    B, L, H, _ = q.shape
    C = min(HG_CHUNK, L)
    n = -(-L // C)
    pad = n * C - L
    f32 = jnp.float32

    def prep(a):
        a = jnp.pad(a.astype(f32), ((0, 0), (0, pad), (0, 0), (0, 0)))
        return a.reshape(B, n, C, H, a.shape[-1]).transpose(1, 0, 3, 2, 4)

    qc, kc, vc, gc = prep(q), prep(k), prep(v), prep(log_f)
    mask = jnp.tril(jnp.ones((C, C), dtype=bool))[:, :, None]

    def step(S, blk):
        qb, kb, vb, gb = blk
        A = jnp.cumsum(gb, axis=2)
        A_last = A[:, :, -1:, :]
        o_inter = jnp.einsum('bhtk,bhkv->bhtv', qb * jnp.exp(A), S)
        decay = jnp.exp(jnp.where(mask, A[:, :, :, None, :] - A[:, :, None, :, :], -jnp.inf))
        scores = jnp.einsum('bhtk,bhtsk,bhsk->bhts', qb, decay, kb)
        o = o_inter + jnp.einsum('bhts,bhsv->bhtv', scores, vb)
        S = jnp.exp(A_last[:, :, 0, :])[..., None] * S + jnp.einsum(
            'bhsk,bhsv->bhkv', kb * jnp.exp(A_last - A), vb)
        return S, o

    S, o = lax.scan(step, s0.astype(f32), (qc, kc, vc, gc))
    o = o.transpose(1, 0, 3, 2, 4).reshape(B, n * C, H, -1)[:, :L]
    return o.astype(v.dtype), S.astype(s0.dtype)


def _layer(x, c, conv_state, s0, lb, ada_w, ada_b, n1, n2, w_in, v_g, w_s, b_s, w_ba,
           o_g, w_bb, w_out, w_up, conv_w, conv_b, w_down):
    B, L, _ = x.shape
    mod = jnp.dot(jax.nn.silu(c), ada_w) + ada_b
    sh1, sc1, g1, sh2, sc2, g2 = [m[:, None, :] for m in jnp.split(mod, 6, axis=-1)]

    h = _rmsnorm(x, n1) * (1 + sc1) + sh1
    z = jnp.einsum('bld,de->ble', h, w_in)
    u, v, q, f, i, og, ga, gb = jnp.split(z, SPLITS, axis=-1)

    u = jax.nn.gelu(u).reshape(B, L, G_A, DG_A)
    v = _rmsnorm(jax.nn.gelu(v).reshape(B, L, G_A, DG_A), v_g.reshape(G_A, DG_A))
    ya = _spatial_gate(u, v, w_s, b_s).reshape(B, L, D_A)
    Lc = min(L, CHUNK)
    start = ((L - 1) // Lc) * Lc
    v_rows = v[:, start:].reshape(B, L - start, D_A)

    q = jax.nn.silu(q).reshape(B, L, H_B, DK) * (DK ** -0.5)
    log_f = jnp.logaddexp(jnp.log(lb), jnp.log1p(-lb) + jax.nn.log_sigmoid(f.astype(jnp.float32)))
    log_f = log_f.reshape(B, L, H_B, DK)
    k = -jnp.expm1(log_f)
    o, s_new = _hgrn2(q, k, i.reshape(B, L, H_B, DV), log_f, s0)
    o = _rmsnorm(o, o_g.reshape(H_B, DV)).reshape(B, L, D_BV) * jax.nn.silu(og)

    y = jax.nn.sigmoid(ga) * jnp.dot(ya, w_ba) + jax.nn.sigmoid(gb) * jnp.dot(o, w_bb)
    x = x + g1 * jnp.dot(y, w_out)

    h = _rmsnorm(x, n2) * (1 + sc2) + sh2
    up = jnp.dot(h, w_up)
    full = jnp.concatenate([conv_state.astype(up.dtype), up], axis=1)
    acc = conv_b
    for j in range(CONV_W):
        acc = acc + full[:, j:j + L] * conv_w[j]
    a, bv = jnp.split(acc, 2, axis=-1)
    x = x + g2 * jnp.dot(jax.nn.gelu(a) * bv, w_down)
    new_conv = full[:, L:]
    return x, new_conv.astype(conv_state.dtype), s_new, v_rows


def setup_inputs(seed: int = 0) -> dict:
    key = jax.random.key(seed)
    ks = jax.random.split(key, 32)

    def nrm(k, shape, scale):
        return jax.random.normal(k, shape, jnp.float32) * scale

    return {
        "x_prompt": nrm(ks[0], (BATCH, SEQ, D_MODEL), 1.0),
        "x_sample": nrm(ks[1], (DEC_BATCH, DEC_SEQ, D_MODEL), 1.0),
        "c_prompt": nrm(ks[2], (BATCH, D_MODEL), 1.0),
        "c_sample": nrm(ks[3], (DEC_BATCH, D_MODEL), 1.0),
        "state_hgrn": nrm(ks[4], (DEPTH, DEC_BATCH, H_B, DK, DV), 0.5),
        "state_ffn_conv": nrm(ks[5], (DEPTH, DEC_BATCH, CONV_W - 1, 2 * D_FF), 1.0),
        "ada_w": nrm(ks[6], (DEPTH, D_MODEL, 6 * D_MODEL), 0.3 * D_MODEL ** -0.5),
        "ada_b": nrm(ks[7], (DEPTH, 6 * D_MODEL), 0.02),
        "norm1_g": 1.0 + nrm(ks[8], (DEPTH, D_MODEL), 0.02),
        "norm2_g": 1.0 + nrm(ks[9], (DEPTH, D_MODEL), 0.02),
        "w_in": nrm(ks[10], (DEPTH, D_MODEL, N_IN), D_MODEL ** -0.5),
        "gmlp_v_g": 1.0 + nrm(ks[11], (DEPTH, D_A), 0.02),
        "gmlp_ws": nrm(ks[12], (DEPTH, G_A, CHUNK, CHUNK), CHUNK ** -0.5),
        "gmlp_bs": 1.0 + nrm(ks[13], (DEPTH, G_A, CHUNK), 0.02),
        "w_branch_a": nrm(ks[14], (DEPTH, D_A, D_MODEL), D_A ** -0.5),
        "hgrn_lower_bounds": nrm(ks[15], (DEPTH, D_BK), 0.5),
        "hgrn_norm_g": 1.0 + nrm(ks[16], (DEPTH, D_BV), 0.02),
        "w_branch_b": nrm(ks[17], (DEPTH, D_BV, D_MODEL), D_BV ** -0.5),
        "w_out": nrm(ks[18], (DEPTH, D_MODEL, D_MODEL), D_MODEL ** -0.5),
        "w_up": nrm(ks[19], (DEPTH, D_MODEL, 2 * D_FF), D_MODEL ** -0.5),
        "conv_w": nrm(ks[20], (DEPTH, CONV_W, 2 * D_FF), CONV_W ** -0.5),
        "conv_b": nrm(ks[21], (DEPTH, 2 * D_FF), 0.02),
        "w_down": nrm(ks[22], (DEPTH, D_FF, D_MODEL), D_FF ** -0.5),
        "final_norm_g": 1.0 + nrm(ks[23], (D_MODEL,), 0.02),
    }


def reference(x_prompt, x_sample, c_prompt, c_sample, state_hgrn, state_ffn_conv,
              ada_w, ada_b, norm1_g, norm2_g, w_in, gmlp_v_g, gmlp_ws, gmlp_bs,
              w_branch_a, hgrn_lower_bounds, hgrn_norm_g, w_branch_b, w_out, w_up,
              conv_w, conv_b, w_down, final_norm_g):
    p = jax.nn.softmax(hgrn_lower_bounds.astype(jnp.float32), axis=0)
    lb_all = jnp.cumsum(p, axis=0)
    lb_all = lb_all - lb_all[0:1]

    conv0 = jnp.zeros((BATCH, CONV_W - 1, 2 * D_FF), state_ffn_conv.dtype)
    s00 = jnp.zeros((BATCH, H_B, DK, DV), state_hgrn.dtype)

    xp, xs = x_prompt, x_sample
    hp, hs, cp, cs, vp, vs = [], [], [], [], [], []
    for l in range(DEPTH):
        w = (ada_w[l], ada_b[l], norm1_g[l], norm2_g[l], w_in[l], gmlp_v_g[l], gmlp_ws[l],
             gmlp_bs[l], w_branch_a[l], hgrn_norm_g[l], w_branch_b[l], w_out[l], w_up[l],
             conv_w[l], conv_b[l], w_down[l])
        xp, c_p, h_p, v_p = _layer(xp, c_prompt, conv0, s00, lb_all[l], *w)
        xs, c_s, h_s, v_s = _layer(xs, c_sample, state_ffn_conv[l], state_hgrn[l], lb_all[l], *w)
        hp.append(h_p); hs.append(h_s); cp.append(c_p); cs.append(c_s); vp.append(v_p); vs.append(v_s)

    y_prompt = _rmsnorm(xp, final_norm_g)
    y_sample = _rmsnorm(xs, final_norm_g)
    return (y_prompt, y_sample, jnp.stack(hp), jnp.stack(hs), jnp.stack(cp), jnp.stack(cs),
            jnp.stack(vp), jnp.stack(vs))
```

```python
import functools

import numpy as np
import jax
import jax.numpy as jnp
from jax import lax
from jax.experimental import pallas as pl
from jax.experimental.pallas import tpu as pltpu

f32 = jnp.float32
bf16 = jnp.bfloat16

D = 2048
BATCH = 4
SEQ = 2048
DEPTH = 2
DEC_BATCH = 128
DEC_SEQ = 8
G_A = 8
DG_A = 128
D_A = G_A * DG_A
CHUNK = 128
H_B = 16
DK = 128
DV = 128
D_FF = 5632
CONV_W = 3
N_IN = 2 * D_A + 6 * D
EPS = 1e-6

T_P = BATCH * SEQ
T_S = DEC_BATCH * DEC_SEQ
T = T_P + T_S
SB = 64
N_SG = DEC_BATCH // SB
TR = DEC_SEQ * SB
N_RB = T // TR
N_PRB = T_P // TR
RB_PER_SEQ = SEQ // TR
TM = 1024
TN = 1024
TK_FF = 512
N_KFF = D_FF // TK_FF
HG_C = 128
HG_NL = 7

_QB, _FB, _IB, _OGB = 16, 32, 48, 64
_MOD_SH1, _MOD_SC1, _MOD_G1, _MOD_SH2, _MOD_SC2, _MOD_G2 = range(6)
N_MODJ = BATCH + N_SG


def _cparams(sem):
    return pltpu.CompilerParams(dimension_semantics=sem)


def _hgrn_tables(c, nl):
    t = np.arange(c)
    r = t[None, :]
    tt = t[:, None]
    blocks = [r <= tt, r > tt]
    lev = np.full((c, c), -1, np.int32)
    lev[t, t] = 0
    for l in range(1, nl + 1):
        h = c >> l
        ref = (t // (2 * h)) * (2 * h) + h - 1
        upper = (t % (2 * h)) >= h
        p = np.where(upper[:, None], (r > ref[:, None]) & (r <= tt), (r > tt) & (r <= ref[:, None]))
        blocks.append(p)
        same = (t[:, None] // (2 * h)) == (t[None, :] // (2 * h))
        lev[same & upper[:, None] & ~upper[None, :]] = l
    return np.concatenate(blocks, 0).astype(np.float32), lev


_PST_NP, _LEV_NP = _hgrn_tables(HG_C, HG_NL)


def _rms(x, eps=EPS):
    return x * lax.rsqrt(jnp.mean(x * x, axis=-1, keepdims=True) + eps)


def _dot(a, b):
    return jnp.dot(a, b, preferred_element_type=f32)


def _dot_nt(a, b):
    return lax.dot_general(a, b, (((1,), (1,)), ((), ())), preferred_element_type=f32)


def _dot_tn(a, b):
    return lax.dot_general(a, b, (((0,), (0,)), ((), ())), preferred_element_type=f32)


def _modj(i):
    return jnp.where(i < N_PRB, i // RB_PER_SEQ, i - N_PRB + BATCH)


def _ada_kernel(c_ref, w_ref, b_ref, o_ref):
    c = c_ref[...]
    a = (c * jax.nn.sigmoid(c)).astype(bf16)
    r = _dot(a, w_ref[...].astype(bf16)) + b_ref[...]
    for g in range(N_SG):
        o_ref[BATCH + g] = r[g * SB:(g + 1) * SB]
    for j in range(BATCH):
        o_ref[j] = jnp.broadcast_to(r[DEC_BATCH + j:DEC_BATCH + j + 1], (SB, TN))


def _ada(c_all, ada_w, ada_b):
    nrow = c_all.shape[0]
    ncol = 6 * D // TN
    per_kind = D // TN
    return pl.pallas_call(
        _ada_kernel,
        out_shape=jax.ShapeDtypeStruct((DEPTH, 6, N_MODJ, SB, D), f32),
        grid=(DEPTH, ncol),
        in_specs=[
            pl.BlockSpec((nrow, D), lambda l, n: (0, 0)),
            pl.BlockSpec((None, D, TN), lambda l, n: (l, 0, n)),
            pl.BlockSpec((None, 1, TN), lambda l, n: (l, 0, n)),
        ],
        out_specs=pl.BlockSpec((None, None, N_MODJ, SB, TN), lambda l, n: (l, n // per_kind, 0, 0, n % per_kind)),
        compiler_params=_cparams(("arbitrary", "arbitrary")),
        name="ada_mod",
    )(c_all, ada_w, ada_b.reshape(DEPTH, 1, 6 * D))


def _mod_spec(layer, kind):
    return pl.BlockSpec((None, None, None, SB, D), lambda i, *_: (layer, kind, _modj(i), 0, 0))


def _norm_mod_kernel(x_ref, g_ref, sh_ref, sc_ref, o_ref):
    g = g_ref[...]
    sh = sh_ref[...]
    sc1 = 1.0 + sc_ref[...]
    for s in range(TR // SB):
        rows = slice(s * SB, (s + 1) * SB)
        o_ref[rows, :] = (_rms(x_ref[rows, :]) * g * sc1 + sh).astype(bf16)


def _norm_mod(x, gain, mods, layer):
    return pl.pallas_call(
        _norm_mod_kernel,
        out_shape=jax.ShapeDtypeStruct((T, D), bf16),
        grid=(N_RB,),
        in_specs=[
            pl.BlockSpec((TR, D), lambda i: (i, 0)),
            pl.BlockSpec((1, D), lambda i: (0, 0)),
            _mod_spec(layer, _MOD_SH1),
            _mod_spec(layer, _MOD_SC1),
        ],
        out_specs=pl.BlockSpec((TR, D), lambda i: (i, 0)),
        compiler_params=_cparams(("arbitrary",)),
        name="norm_mod",
    )(x, gain.reshape(1, D), mods, mods)


def _inproj_kernel(h_ref, w_ref, lb_ref, vg_ref, o_ref, wbf_ref, *, layer):
    n = pl.program_id(0)
    m = pl.program_id(1)

    @pl.when(m == 0)
    def _():
        wbf_ref[...] = w_ref[...].astype(bf16)

    acc = _dot(h_ref[...], wbf_ref[...])

    @pl.when(n == 0)
    def _():
        o_ref[...] = jax.nn.gelu(acc)

    @pl.when(n == 1)
    def _():
        v = jax.nn.gelu(acc)
        for g in range(G_A):
            cols = slice(g * DG_A, (g + 1) * DG_A)
            o_ref[:, cols] = _rms(v[:, cols]) * vg_ref[:, cols]

    @pl.when((n >= 2) & (n < 4))
    def _():
        o_ref[...] = acc * jax.nn.sigmoid(acc) * (DK ** -0.5)

    @pl.when((n >= 4) & (n < 6))
    def _():
        p = jax.nn.softmax(lb_ref[...], axis=0)
        lb = jnp.zeros((1, TN), f32)
        for j in range(1, layer + 1):
            lb = lb + p[j:j + 1]
        o_ref[...] = jnp.logaddexp(jnp.log(lb), jnp.log1p(-lb) + jax.nn.log_sigmoid(acc))

    @pl.when((n >= 6) & (n < 8))
    def _():
        o_ref[...] = acc

    @pl.when((n >= 8) & (n < 10))
    def _():
        o_ref[...] = acc * jax.nn.sigmoid(acc)

    @pl.when(n >= 10)
    def _():
        o_ref[...] = jax.nn.sigmoid(acc)


def _inproj(h, w_in, lower_bounds, v_gain, layer):
    return pl.pallas_call(
        functools.partial(_inproj_kernel, layer=layer),
        out_shape=jax.ShapeDtypeStruct((T, N_IN), f32),
        grid=(N_IN // TN, T // TM),
        in_specs=[
            pl.BlockSpec((TM, D), lambda n, m: (m, 0)),
            pl.BlockSpec((None, D, TN), lambda n, m: (layer, 0, n)),
            pl.BlockSpec((DEPTH, TN), lambda n, m: (0, jnp.clip(n - 4, 0, 1))),
            pl.BlockSpec((None, 1, D_A), lambda n, m: (layer, 0, 0)),
        ],
        out_specs=pl.BlockSpec((TM, TN), lambda n, m: (m, n)),
        scratch_shapes=[pltpu.VMEM((D, TN), bf16)],
        compiler_params=_cparams(("arbitrary", "arbitrary")),
        name=f"inproj{layer}",
    )(h, w_in, lower_bounds, v_gain.reshape(DEPTH, 1, D_A))


def _gmlp_kernel(u_ref, v_ref, ga_ref, ws_ref, bias_ref, w8_ref, b8_ref, wba_ref, o_ref, wm_ref, ya_ref):
    i = pl.program_id(0)

    @pl.when(i == 0)
    def _():
        r = lax.broadcasted_iota(jnp.int32, (CHUNK, CHUNK), 0)
        c = lax.broadcasted_iota(jnp.int32, (CHUNK, CHUNK), 1)
        for g in range(G_A):
            wm_ref[g] = jnp.where(r >= c, ws_ref[g], 0.0).astype(bf16)

    @pl.when(i < N_PRB)
    def _():
        for ch in range(TR // CHUNK):
            rows = slice(ch * CHUNK, (ch + 1) * CHUNK)
            for g in range(G_A):
                cols = slice(g * DG_A, (g + 1) * DG_A)
                s = _dot(wm_ref[g], v_ref[rows, cols].astype(bf16)) + bias_ref[:, cols]
                ya_ref[rows, cols] = (u_ref[rows, cols] * s).astype(bf16)

    @pl.when(i >= N_PRB)
    def _():
        for g in range(G_A):
            cols = slice(g * DG_A, (g + 1) * DG_A)
            vs = [v_ref[s * SB:(s + 1) * SB, cols] for s in range(DEC_SEQ)]
            for t in range(DEC_SEQ):
                acc = vs[0] * w8_ref[g * 64 + t * 8]
                for s in range(1, t + 1):
                    acc = acc + vs[s] * w8_ref[g * 64 + t * 8 + s]
                acc = acc + b8_ref[g * 8 + t]
                ya_ref[t * SB:(t + 1) * SB, cols] = (u_ref[t * SB:(t + 1) * SB, cols] * acc).astype(bf16)

    o_ref[...] = ga_ref[...] * _dot(ya_ref[...], wba_ref[...])


def _gmlp(z, ws, bs, wba_bf, layer):
    bias_full = jnp.repeat(bs.T, DG_A, axis=1)
    w8 = ws[:, :DEC_SEQ, :DEC_SEQ].reshape(G_A * DEC_SEQ * DEC_SEQ)
    b8 = bs[:, :DEC_SEQ].reshape(G_A * DEC_SEQ)
    return pl.pallas_call(
        _gmlp_kernel,
        out_shape=jax.ShapeDtypeStruct((T, D), f32),
        grid=(N_RB,),
        in_specs=[
            pl.BlockSpec((TR, D_A), lambda i: (i, 0)),
            pl.BlockSpec((TR, D_A), lambda i: (i, 1)),
            pl.BlockSpec((TR, D), lambda i: (i, 5)),
            pl.BlockSpec((G_A, CHUNK, CHUNK), lambda i: (0, 0, 0)),
            pl.BlockSpec((CHUNK, D_A), lambda i: (0, 0)),
            pl.BlockSpec(memory_space=pltpu.SMEM),
            pl.BlockSpec(memory_space=pltpu.SMEM),
            pl.BlockSpec((D_A, D), lambda i: (0, 0)),
        ],
        out_specs=pl.BlockSpec((TR, D), lambda i: (i, 0)),
        scratch_shapes=[pltpu.VMEM((G_A, CHUNK, CHUNK), bf16), pltpu.VMEM((TR, D_A), bf16)],
        compiler_params=_cparams(("arbitrary",)),
        name=f"gmlp{layer}",
    )(z, z, z, ws, bias_full, w8, b8, wba_bf)


def _hgrn_prompt_kernel(q_ref, g_ref, v_ref, og_ref, pst_ref, lev_ref, gain_ref, on_ref, so_ref, st_ref):
    c_len = HG_C
    st_ref[...] = jnp.zeros((DV, DK), f32)
    gain = gain_ref[...]

    def chunk(c, carry):
        r = pl.ds(pl.multiple_of(c * c_len, c_len), c_len)
        q = q_ref[r, :]
        g = g_ref[r, :]
        v = v_ref[r, :].astype(bf16)
        g1 = g.astype(bf16)
        r1 = g - g1.astype(f32)
        g2 = r1.astype(bf16)
        g3 = (r1 - g2.astype(f32)).astype(bf16)
        gs = _dot(pst_ref[...], jnp.concatenate([g1, g2, g3], axis=1))
        gs = gs[:, :DK] + gs[:, DK:2 * DK] + gs[:, 2 * DK:]
        a = gs[0:c_len]
        b = gs[c_len:2 * c_len]
        kk = 1.0 - jnp.exp(g)
        st = st_ref[...]
        o = _dot_nt((q * jnp.exp(a)).astype(bf16), st.astype(bf16))
        lev = lev_ref[...]
        sc = jnp.where(lev == 0, _dot_nt(q.astype(bf16), kk.astype(bf16)), 0.0)
        for l in range(1, HG_NL + 1):
            e = jnp.exp(gs[(1 + l) * c_len:(2 + l) * c_len])
            sc = sc + jnp.where(lev == l, _dot_nt((q * e).astype(bf16), (kk * e).astype(bf16)), 0.0)
        o = o + _dot(sc.astype(bf16), v)
        st_ref[...] = jnp.exp(a[c_len - 1:c_len, :]) * st + _dot_tn(v, (kk * jnp.exp(b)).astype(bf16))
        on_ref[r, :] = (_rms(o) * gain * og_ref[r, :]).astype(bf16)
        return carry

    lax.fori_loop(0, SEQ // c_len, chunk, 0)
    so_ref[...] = st_ref[...].T


def _hgrn_prompt(z, gain, layer):
    zspec = lambda off: pl.BlockSpec((SEQ, DK), lambda b, h: (b, off + h))
    nrow = _PST_NP.shape[0]
    return pl.pallas_call(
        _hgrn_prompt_kernel,
        out_shape=(jax.ShapeDtypeStruct((T, D), bf16), jax.ShapeDtypeStruct((BATCH, H_B, DK, DV), f32)),
        grid=(BATCH, H_B),
        in_specs=[
            zspec(_QB), zspec(_FB), zspec(_IB), zspec(_OGB),
            pl.BlockSpec((nrow, HG_C), lambda b, h: (0, 0)),
            pl.BlockSpec((HG_C, HG_C), lambda b, h: (0, 0)),
            pl.BlockSpec((None, 1, DV), lambda b, h: (layer, 0, h)),
        ],
        out_specs=(pl.BlockSpec((SEQ, DV), lambda b, h: (b, h)),
                   pl.BlockSpec((None, None, DK, DV), lambda b, h: (b, h, 0, 0))),
        scratch_shapes=[pltpu.VMEM((DV, DK), f32)],
        compiler_params=_cparams(("arbitrary", "arbitrary")),
        name=f"hgrn_prompt{layer}",
    )(z, z, z, z, jnp.asarray(_PST_NP, bf16), jnp.asarray(_LEV_NP), gain.reshape(DEPTH, 1, D))


def _hgrn_sample_kernel(*refs, aliased):
    q_ref, g_ref, v_ref, og_ref, s_ref, gain_ref = refs[:6]
    on_ref, so_ref, o_scr = refs[-3:]
    gain = gain_ref[...]
    tio = lax.broadcasted_iota(jnp.int32, (DEC_SEQ, DK), 0)

    def per_batch(b, carry):
        rows = pl.ds(b, DEC_SEQ, stride=SB)
        q = q_ref[rows, :]
        g = g_ref[rows, :]
        v = v_ref[rows, :]
        a = g
        for sh in (1, 2, 4):
            a = a + jnp.where(tio >= sh, pltpu.roll(a, sh, 0), 0.0)
        kk = 1.0 - jnp.exp(g)
        s0 = s_ref[b]
        o = _dot((q * jnp.exp(a)).astype(bf16), s0.astype(bf16))
        for s in range(DEC_SEQ):
            d = jnp.where(tio >= s, a - a[s:s + 1, :], -jnp.inf)
            w = jnp.sum(jnp.exp(d) * q * kk[s:s + 1, :], axis=-1, keepdims=True)
            o = o + w * v[s:s + 1, :]
        a_last = a[DEC_SEQ - 1:DEC_SEQ, :]
        kb = (kk * jnp.exp(a_last - a)).astype(bf16)
        f_col = jnp.transpose(jnp.exp(a))[:, DEC_SEQ - 1:DEC_SEQ]
        so_ref[b] = f_col * s0 + _dot_tn(kb, v.astype(bf16))
        o_scr[rows, :] = _rms(o) * gain * og_ref[rows, :]
        return carry

    lax.fori_loop(0, SB, per_batch, 0)
    on_ref[...] = o_scr[...].astype(bf16)


def _hgrn_sample(z, on, state_hgrn, hs_prev, gain, layer):
    rb0 = T_P // TR
    zspec = lambda off: pl.BlockSpec((TR, DK), lambda g, h: (rb0 + g, off + h))
    sspec = pl.BlockSpec((None, SB, None, DK, DV), lambda g, h: (layer, g, h, 0, 0))
    aliased = hs_prev is not None
    ins = [z, z, z, z, state_hgrn, gain.reshape(DEPTH, 1, D), on]
    in_specs = [zspec(_QB), zspec(_FB), zspec(_IB), zspec(_OGB), sspec,
                pl.BlockSpec((None, 1, DV), lambda g, h: (layer, 0, h)),
                pl.BlockSpec(memory_space=pl.ANY)]
    aliases = {6: 0}
    if aliased:
        ins.append(hs_prev)
        in_specs.append(pl.BlockSpec(memory_space=pl.ANY))
        aliases[7] = 1
    return pl.pallas_call(
        functools.partial(_hgrn_sample_kernel, aliased=aliased),
        out_shape=(jax.ShapeDtypeStruct((T, D), bf16), jax.ShapeDtypeStruct(state_hgrn.shape, f32)),
        grid=(N_SG, H_B),
        in_specs=in_specs,
        out_specs=(pl.BlockSpec((TR, DV), lambda g, h: (rb0 + g, h)), sspec),
        scratch_shapes=[pltpu.VMEM((TR, DV), f32)],
        input_output_aliases=aliases,
        compiler_params=_cparams(("arbitrary", "arbitrary")),
        name=f"hgrn_sample{layer}",
    )(*ins)


def _mix_kernel(on_ref, w_ref, a_ref, gb_ref, o_ref, wbf_ref):
    @pl.when(pl.program_id(1) == 0)
    def _():
        wbf_ref[...] = w_ref[...].astype(bf16)

    o_ref[...] = (a_ref[...] + gb_ref[...] * _dot(on_ref[...], wbf_ref[...])).astype(bf16)


def _branch_mix(on, w_bb, a_gated, z, layer):
    gb0 = (N_IN - D) // TN
    return pl.pallas_call(
        _mix_kernel,
        out_shape=jax.ShapeDtypeStruct((T, D), bf16),
        grid=(D // TN, T // TM),
        in_specs=[
            pl.BlockSpec((TM, D), lambda n, m: (m, 0)),
            pl.BlockSpec((None, D, TN), lambda n, m: (layer, 0, n)),
            pl.BlockSpec((TM, TN), lambda n, m: (m, n)),
            pl.BlockSpec((TM, TN), lambda n, m: (m, gb0 + n)),
        ],
        out_specs=pl.BlockSpec((TM, TN), lambda n, m: (m, n)),
        scratch_shapes=[pltpu.VMEM((D, TN), bf16)],
        compiler_params=_cparams(("arbitrary", "arbitrary")),
        name=f"branch_mix{layer}",
    )(on, w_bb, a_gated, z)


def _outproj_kernel(y_ref, w_ref, x_ref, n2_ref, g1_ref, sh_ref, sc_ref, xo_ref, h_ref):
    xo_ref[...] = _dot(y_ref[...], w_ref[...])
    n2 = n2_ref[...]
    g1 = g1_ref[...]
    sh = sh_ref[...]
    sc1 = 1.0 + sc_ref[...]
    for s in range(TR // SB):
        rows = slice(s * SB, (s + 1) * SB)
        xm = x_ref[rows, :] + g1 * xo_ref[rows, :]
        xo_ref[rows, :] = xm
        h_ref[rows, :] = (_rms(xm) * n2 * sc1 + sh).astype(bf16)


def _outproj(y, w_out_bf, x, n2, mods, layer):
    return pl.pallas_call(
        _outproj_kernel,
        out_shape=(jax.ShapeDtypeStruct((T, D), f32), jax.ShapeDtypeStruct((T, D), bf16)),
        grid=(N_RB,),
        in_specs=[
            pl.BlockSpec((TR, D), lambda i: (i, 0)),
            pl.BlockSpec((None, D, D), lambda i: (layer, 0, 0), pipeline_mode=pl.Buffered(1)),
            pl.BlockSpec((TR, D), lambda i: (i, 0)),
            pl.BlockSpec((None, 1, D), lambda i: (layer, 0, 0)),
            _mod_spec(layer, _MOD_G1), _mod_spec(layer, _MOD_SH2), _mod_spec(layer, _MOD_SC2),
        ],
        out_specs=(pl.BlockSpec((TR, D), lambda i: (i, 0)), pl.BlockSpec((TR, D), lambda i: (i, 0))),
        compiler_params=_cparams(("arbitrary",)),
        name=f"outproj{layer}",
    )(y, w_out_bf, x, n2.reshape(DEPTH, 1, D), mods, mods, mods)


def _upproj_kernel(h_ref, w_ref, o_ref, wbf_ref):
    @pl.when(pl.program_id(1) == 0)
    def _():
        wbf_ref[...] = w_ref[...].astype(bf16)

    o_ref[...] = _dot(h_ref[...], wbf_ref[...])


def _upproj(h, w_up, layer):
    return pl.pallas_call(
        _upproj_kernel,
        out_shape=jax.ShapeDtypeStruct((T, 2 * D_FF), f32),
        grid=(2 * D_FF // TN, T // TM),
        in_specs=[
            pl.BlockSpec((TM, D), lambda n, m: (m, 0)),
            pl.BlockSpec((None, D, TN), lambda n, m: (layer, 0, n)),
        ],
        out_specs=pl.BlockSpec((TM, TN), lambda n, m: (m, n)),
        scratch_shapes=[pltpu.VMEM((D, TN), bf16)],
        compiler_params=_cparams(("arbitrary", "arbitrary")),
        name=f"upproj{layer}",
    )(h, w_up)


def _conv3(cur, m1, m2, cw_ref, cb_ref):
    return cb_ref[...] + cw_ref[0:1, :] * m2 + cw_ref[1:2, :] * m1 + cw_ref[2:3, :] * cur


def _ffn_kernel(a_ref, b_ref, ha_ref, hb_ref, sa_ref, sb_ref, cwa_ref, cwb_ref, cba_ref, cbb_ref, wd_ref,
                xm_ref, g2_ref, gain_ref, sh_ref, sc_ref, xo_ref, ho_ref, acc_ref, act_ref, *, final):
    i = pl.program_id(0)
    k = pl.program_id(1)

    @pl.when(i < N_PRB)
    def _():
        seq_start = (i % RB_PER_SEQ) == 0
        r8 = lax.broadcasted_iota(jnp.int32, (8, TK_FF), 0)

        def conv(cur_ref, halo_ref, cw_ref, cb_ref):
            cur = cur_ref[...]
            body = _conv3(cur, pltpu.roll(cur, 1, 0), pltpu.roll(cur, 2, 0), cw_ref, cb_ref)
            halo = jnp.where(seq_start, 0.0, halo_ref[...])
            cur8 = cur_ref[0:8, :]
            m1 = jnp.where(r8 < 1, pltpu.roll(halo, 1, 0), pltpu.roll(cur8, 1, 0))
            m2 = jnp.where(r8 < 2, pltpu.roll(halo, 2, 0), pltpu.roll(cur8, 2, 0))
            return body, _conv3(cur8, m1, m2, cw_ref, cb_ref)

        ca, ca8 = conv(a_ref, ha_ref, cwa_ref, cba_ref)
        cb, cb8 = conv(b_ref, hb_ref, cwb_ref, cbb_ref)
        act_ref[...] = (jax.nn.gelu(ca) * cb).astype(bf16)
        act_ref[0:8, :] = (jax.nn.gelu(ca8) * cb8).astype(bf16)

    @pl.when(i >= N_PRB)
    def _():
        def conv(cur_ref, st_ref, cw_ref, cb_ref, t):
            slab = lambda j: cur_ref[j * SB:(j + 1) * SB, :]
            m1 = slab(t - 1) if t >= 1 else st_ref[1]
            m2 = slab(t - 2) if t >= 2 else st_ref[t]
            return _conv3(slab(t), m1, m2, cw_ref, cb_ref)

        for t in range(DEC_SEQ):
            ca = conv(a_ref, sa_ref, cwa_ref, cba_ref, t)
            cb = conv(b_ref, sb_ref, cwb_ref, cbb_ref, t)
            act_ref[t * SB:(t + 1) * SB, :] = (jax.nn.gelu(ca) * cb).astype(bf16)

    part = _dot(act_ref[...], wd_ref[...])

    @pl.when(k == 0)
    def _():
        acc_ref[...] = part

    @pl.when(k > 0)
    def _():
        acc_ref[...] += part

    @pl.when(k == N_KFF - 1)
    def _():
        g2 = g2_ref[...]
        gain = gain_ref[...]
        for s in range(TR // SB):
            rows = slice(s * SB, (s + 1) * SB)
            xn = xm_ref[rows, :] + g2 * acc_ref[rows, :]
            xo_ref[rows, :] = xn
            if final:
                ho_ref[rows, :] = _rms(xn) * gain
            else:
                ho_ref[rows, :] = (_rms(xn) * gain * (1.0 + sc_ref[...]) + sh_ref[...]).astype(bf16)


def _ffn(up, conv_state, conv_w, conv_b, wd_bf, x_mid, mods, gain, layer, final):
    nb = D_FF // TK_FF
    halo = lambda off: pl.BlockSpec((8, TK_FF), lambda i, k: (jnp.maximum(i * (TR // 8) - 1, 0), off + k))
    st = lambda off: pl.BlockSpec((2, None, SB, TK_FF), lambda i, k: (0, jnp.clip(i - N_PRB, 0, N_SG - 1), 0, off + k))
    cw = lambda off: pl.BlockSpec((None, CONV_W, TK_FF), lambda i, k: (layer, 0, off + k))
    cb = lambda off: pl.BlockSpec((None, 1, TK_FF), lambda i, k: (layer, 0, off + k))
    nxt = layer if final else layer + 1
    return pl.pallas_call(
        functools.partial(_ffn_kernel, final=final),
        out_shape=(jax.ShapeDtypeStruct((T, D), f32), jax.ShapeDtypeStruct((T, D), f32 if final else bf16)),
        grid=(N_RB, N_KFF),
        in_specs=[
            pl.BlockSpec((TR, TK_FF), lambda i, k: (i, k)),
            pl.BlockSpec((TR, TK_FF), lambda i, k: (i, nb + k)),
            halo(0), halo(nb), st(0), st(nb), cw(0), cw(nb), cb(0), cb(nb),
            pl.BlockSpec((None, TK_FF, D), lambda i, k: (layer, k, 0)),
            pl.BlockSpec((TR, D), lambda i, k: (i, 0)),
            _mod_spec(layer, _MOD_G2),
            pl.BlockSpec((1, D), lambda i, k: (0, 0)),
            _mod_spec(nxt, _MOD_SH1), _mod_spec(nxt, _MOD_SC1),
        ],
        out_specs=(pl.BlockSpec((TR, D), lambda i, k: (i, 0)), pl.BlockSpec((TR, D), lambda i, k: (i, 0))),
        scratch_shapes=[pltpu.VMEM((TR, D), f32), pltpu.VMEM((TR, TK_FF), bf16)],
        compiler_params=_cparams(("arbitrary", "arbitrary")),
        name=f"ffn{layer}",
    )(up, up, up, up, conv_state, conv_state, conv_w, conv_w,
      conv_b.reshape(DEPTH, 1, 2 * D_FF), conv_b.reshape(DEPTH, 1, 2 * D_FF),
      wd_bf, x_mid, mods, gain.reshape(1, D), mods, mods)


def _to_rows(a):
    f = a.shape[-1]
    return a.reshape(N_SG, SB, DEC_SEQ, f).transpose(0, 2, 1, 3).reshape(T_S, f)


def _from_rows(a, steps=DEC_SEQ):
    f = a.shape[-1]
    return a.reshape(N_SG, steps, SB, f).transpose(0, 2, 1, 3).reshape(DEC_BATCH, steps, f)


def kernel(x_prompt, x_sample, c_prompt, c_sample, state_hgrn, state_ffn_conv, ada_w, ada_b, norm1_g, norm2_g,
           w_in, gmlp_v_g, gmlp_ws, gmlp_bs, w_branch_a, hgrn_lower_bounds, hgrn_norm_g, w_branch_b, w_out,
           w_up, conv_w, conv_b, w_down, final_norm_g):
    x = jnp.concatenate([x_prompt.reshape(T_P, D), _to_rows(x_sample)], axis=0)
    c_all = jnp.concatenate([c_sample, c_prompt, jnp.zeros((4, D), f32)], axis=0)
    mods = _ada(c_all, ada_w, ada_b)

    wba_bf = w_branch_a.astype(bf16)
    wout_bf = w_out.astype(bf16)
    wd_bf = w_down.astype(bf16)
    conv_st = state_ffn_conv.transpose(0, 2, 1, 3).reshape(DEPTH, CONV_W - 1, N_SG, SB, 2 * D_FF)

    h = _norm_mod(x, norm1_g[0], mods, 0)
    hs_all = None
    hp, cp, cs, vp, vs = [], [], [], [], []
    y = None
    for l in range(DEPTH):
        z = _inproj(h, w_in, hgrn_lower_bounds, gmlp_v_g, l)
        a_gated = _gmlp(z, gmlp_ws[l], gmlp_bs[l], wba_bf[l], l)
        on, s_p = _hgrn_prompt(z, hgrn_norm_g, l)
        on, hs_all = _hgrn_sample(z, on, state_hgrn, hs_all, hgrn_norm_g, l)
        ymix = _branch_mix(on, w_branch_b, a_gated, z, l)
        x_mid, h2 = _outproj(ymix, wout_bf, x, norm2_g, mods, l)
        up = _upproj(h2, w_up, l)
        final = l == DEPTH - 1
        gain = final_norm_g if final else norm1_g[l + 1]
        x, hn = _ffn(up, conv_st[l], conv_w, conv_b, wd_bf, x_mid, mods, gain, l, final)
        if final:
            y = hn
        else:
            h = hn
        hp.append(s_p)
        cp.append(up[:T_P].reshape(BATCH, SEQ, 2 * D_FF)[:, SEQ - (CONV_W - 1):])
        cs.append(_from_rows(up[T_P:].reshape(N_SG, DEC_SEQ, SB, 2 * D_FF)[:, DEC_SEQ - (CONV_W - 1):]
                             .reshape(N_SG * (CONV_W - 1) * SB, 2 * D_FF), CONV_W - 1))
        vp.append(z[:T_P, D_A:2 * D_A].reshape(BATCH, SEQ, D_A)[:, SEQ - CHUNK:])
        vs.append(_from_rows(z[T_P:, D_A:2 * D_A]))

    y_prompt = y[:T_P].reshape(BATCH, SEQ, D)
    y_sample = _from_rows(y[T_P:])
    return (y_prompt, y_sample, jnp.stack(hp), hs_all, jnp.stack(cp), jnp.stack(cs), jnp.stack(vp), jnp.stack(vs))
```

```python
import functools

import numpy as np
import jax
import jax.numpy as jnp
from jax import lax
from jax.experimental import pallas as pl
from jax.experimental.pallas import tpu as pltpu

f32 = jnp.float32
bf16 = jnp.bfloat16

D = 2048
BATCH = 4
SEQ = 2048
DEPTH = 2
DEC_BATCH = 128
DEC_SEQ = 8
G_A = 8
DG_A = 128
D_A = G_A * DG_A
CHUNK = 128
H_B = 16
DK = 128
DV = 128
D_FF = 5632
CONV_W = 3
N_IN = 2 * D_A + 6 * D
EPS = 1e-6

T_P = BATCH * SEQ
T_S = DEC_BATCH * DEC_SEQ
T = T_P + T_S
SB = 64
N_SG = DEC_BATCH // SB
TR = DEC_SEQ * SB
N_RB = T // TR
N_PRB = T_P // TR
RB_PER_SEQ = SEQ // TR
TM = 1024
N_MB = T // TM
N_PMB = T_P // TM
MB_PER_SEQ = SEQ // TM
TN = 1024
TC_FF = 512
N_CFF = D_FF // TC_FF
TR_FF = 256
HG_C = 128
HG_NL = 7
TAIL_ROWS = 8 * 16 + (CONV_W - 1) * DEC_BATCH

_U0, _V0, _Q0, _F0, _I0, _OG0, _GA0 = 0, 1, 2, 4, 6, 8, 10
_MOD_SH1, _MOD_SC1, _MOD_G1, _MOD_SH2, _MOD_SC2, _MOD_G2 = range(6)
N_MODJ = BATCH + N_SG


def _cparams(sem):
    return pltpu.CompilerParams(dimension_semantics=sem)


def _hgrn_tables(c, nl):
    t = np.arange(c)
    r = t[None, :]
    tt = t[:, None]
    blocks = [r <= tt, r > tt]
    lev = np.full((c, c), -1, np.int32)
    lev[t, t] = 0
    for l in range(1, nl + 1):
        h = c >> l
        ref = (t // (2 * h)) * (2 * h) + h - 1
        upper = (t % (2 * h)) >= h
        p = np.where(upper[:, None], (r > ref[:, None]) & (r <= tt), (r > tt) & (r <= ref[:, None]))
        blocks.append(p)
        same = (t[:, None] // (2 * h)) == (t[None, :] // (2 * h))
        lev[same & upper[:, None] & ~upper[None, :]] = l
    return np.concatenate(blocks, 0).astype(np.float32), lev


_PST_NP, _LEV_NP = _hgrn_tables(HG_C, HG_NL)


def _rms(x, eps=EPS):
    return x * lax.rsqrt(jnp.mean(x * x, axis=-1, keepdims=True) + eps)


def _dot(a, b):
    return jnp.dot(a, b, preferred_element_type=f32)


def _dot_nt(a, b):
    return lax.dot_general(a, b, (((1,), (1,)), ((), ())), preferred_element_type=f32)


def _dot_tn(a, b):
    return lax.dot_general(a, b, (((0,), (0,)), ((), ())), preferred_element_type=f32)


def _modj(i, rows):
    return jnp.where(i < T_P // rows, i // (SEQ // rows), BATCH + (i - T_P // rows) // (TR // rows))


def _mod_spec(layer, kind, rows=TR):
    return pl.BlockSpec((None, None, None, SB, D), lambda i, *_: (layer, kind, _modj(i, rows), 0, 0))


def _ada_kernel(c_ref, w_ref, b_ref, o_ref):
    c = c_ref[...]
    a = (c * jax.nn.sigmoid(c)).astype(bf16)
    r = _dot(a, w_ref[...].astype(bf16)) + b_ref[...]
    for g in range(N_SG):
        o_ref[BATCH + g] = r[g * SB:(g + 1) * SB]
    for j in range(BATCH):
        o_ref[j] = jnp.broadcast_to(r[DEC_BATCH + j:DEC_BATCH + j + 1], (SB, TN))


def _ada(c_all, ada_w, ada_b):
    nrow = c_all.shape[0]
    ncol = 6 * D // TN
    per_kind = D // TN
    return pl.pallas_call(
        _ada_kernel,
        out_shape=jax.ShapeDtypeStruct((DEPTH, 6, N_MODJ, SB, D), f32),
        grid=(DEPTH, ncol),
        in_specs=[
            pl.BlockSpec((nrow, D), lambda l, n: (0, 0)),
            pl.BlockSpec((None, D, TN), lambda l, n: (l, 0, n)),
            pl.BlockSpec((None, 1, TN), lambda l, n: (l, 0, n)),
        ],
        out_specs=pl.BlockSpec((None, None, N_MODJ, SB, TN), lambda l, n: (l, n // per_kind, 0, 0, n % per_kind)),
        compiler_params=_cparams(("arbitrary", "arbitrary")),
        name="ada_mod",
    )(c_all, ada_w, ada_b.reshape(DEPTH, 1, 6 * D))


def _norm_mod_kernel(x_ref, g_ref, sh_ref, sc_ref, o_ref):
    g = g_ref[...]
    sh = sh_ref[...]
    sc1 = 1.0 + sc_ref[...]
    for s in range(TR // SB):
        rows = slice(s * SB, (s + 1) * SB)
        o_ref[rows, :] = (_rms(x_ref[rows, :]) * g * sc1 + sh).astype(bf16)


def _norm_mod(x, gain, mods, layer):
    return pl.pallas_call(
        _norm_mod_kernel,
        out_shape=jax.ShapeDtypeStruct((T, D), bf16),
        grid=(N_RB,),
        in_specs=[
            pl.BlockSpec((TR, D), lambda i: (i, 0)),
            pl.BlockSpec((1, D), lambda i: (0, 0)),
            _mod_spec(layer, _MOD_SH1),
            _mod_spec(layer, _MOD_SC1),
        ],
        out_specs=pl.BlockSpec((TR, D), lambda i: (i, 0)),
        compiler_params=_cparams(("arbitrary",)),
        name="norm_mod",
    )(x, gain.reshape(1, D), mods, mods)


def _proj_kernel(h_ref, w_ref, *rest, epilogue):
    *extra, o_ref, wbf_ref = rest

    @pl.when(pl.program_id(1) == 0)
    def _():
        wbf_ref[...] = w_ref[...].astype(bf16)

    epilogue(_dot(h_ref[...], wbf_ref[...]), o_ref, *extra)


def _epi_gelu(acc, o_ref):
    o_ref[...] = jax.nn.gelu(acc).astype(o_ref.dtype)


def _epi_v(acc, o_ref, vg_ref):
    v = jax.nn.gelu(acc)
    for g in range(G_A):
        cols = slice(g * DG_A, (g + 1) * DG_A)
        o_ref[:, cols] = _rms(v[:, cols]) * vg_ref[:, cols]


def _epi_silu(acc, o_ref, scale_ref):
    o_ref[...] = (acc * jax.nn.sigmoid(acc) * scale_ref[pl.program_id(0)]).astype(o_ref.dtype)


def _epi_logf(acc, o_ref, lb_ref, *, layer):
    p = jax.nn.softmax(lb_ref[...], axis=0)
    lb = jnp.zeros((1, TN), f32)
    for j in range(1, layer + 1):
        lb = lb + p[j:j + 1]
    o_ref[...] = jnp.logaddexp(jnp.log(lb), jnp.log1p(-lb) + jax.nn.log_sigmoid(acc))


def _epi_id(acc, o_ref):
    o_ref[...] = acc.astype(o_ref.dtype)


def _epi_sigmoid(acc, o_ref):
    o_ref[...] = jax.nn.sigmoid(acc).astype(o_ref.dtype)


def _proj(name, h, w_in, layer, col_of, n_tiles, out_dtype, epilogue, extra=(), extra_specs=()):
    return pl.pallas_call(
        functools.partial(_proj_kernel, epilogue=epilogue),
        out_shape=jax.ShapeDtypeStruct((T, n_tiles * TN), out_dtype),
        grid=(n_tiles, N_MB),
        in_specs=[
            pl.BlockSpec((TM, D), lambda n, m: (m, 0)),
            pl.BlockSpec((None, D, TN), lambda n, m: (layer, 0, col_of(n))),
            *extra_specs,
        ],
        out_specs=pl.BlockSpec((TM, TN), lambda n, m: (m, n)),
        scratch_shapes=[pltpu.VMEM((D, TN), bf16)],
        compiler_params=_cparams(("arbitrary", "arbitrary")),
        name=f"{name}{layer}",
    )(h, w_in, *extra)


def _inproj(h, w_in, lower_bounds, v_gain, layer):
    smem = pl.BlockSpec(memory_space=pltpu.SMEM)
    zu = _proj("in_u", h, w_in, layer, lambda n: _U0 + n, 1, bf16, _epi_gelu)
    zv = _proj("in_v", h, w_in, layer, lambda n: _V0 + n, 1, f32, _epi_v,
               (v_gain.reshape(DEPTH, 1, D_A),), (pl.BlockSpec((None, 1, D_A), lambda n, m: (layer, 0, 0)),))
    scales = jnp.asarray([DK ** -0.5] * 2 + [1.0] * 2, f32)
    zqo = _proj("in_qo", h, w_in, layer, lambda n: jnp.where(n < 2, _Q0 + n, _OG0 + n - 2), 4, bf16, _epi_silu,
                (scales,), (smem,))
    zf = _proj("in_f", h, w_in, layer, lambda n: _F0 + n, 2, f32, functools.partial(_epi_logf, layer=layer),
               (lower_bounds,), (pl.BlockSpec((DEPTH, TN), lambda n, m: (0, n)),))
    zi = _proj("in_i", h, w_in, layer, lambda n: _I0 + n, 2, bf16, _epi_id)
    zg = _proj("in_g", h, w_in, layer, lambda n: _GA0 + n, 4, bf16, _epi_sigmoid)
    return zu, zv, zqo, zf, zi, zg


def _gmlp_kernel(u_ref, v_ref, ga_ref, ws_ref, bias_ref, w8_ref, b8_ref, wba_ref, o_ref, wm_ref, ya_ref):
    i = pl.program_id(0)

    @pl.when(i == 0)
    def _():
        r = lax.broadcasted_iota(jnp.int32, (CHUNK, CHUNK), 0)
        c = lax.broadcasted_iota(jnp.int32, (CHUNK, CHUNK), 1)
        for g in range(G_A):
            wm_ref[g] = jnp.where(r >= c, ws_ref[g], 0.0).astype(bf16)

    @pl.when(i < N_PRB)
    def _():
        for ch in range(TR // CHUNK):
            rows = slice(ch * CHUNK, (ch + 1) * CHUNK)
            for g in range(G_A):
                cols = slice(g * DG_A, (g + 1) * DG_A)
                s = _dot(wm_ref[g], v_ref[rows, cols].astype(bf16)) + bias_ref[:, cols]
                ya_ref[rows, cols] = (u_ref[rows, cols].astype(f32) * s).astype(bf16)

    @pl.when(i >= N_PRB)
    def _():
        for g in range(G_A):
            cols = slice(g * DG_A, (g + 1) * DG_A)
            vs = [v_ref[s * SB:(s + 1) * SB, cols] for s in range(DEC_SEQ)]
            for t in range(DEC_SEQ):
                acc = vs[0] * w8_ref[g * 64 + t * 8]
                for s in range(1, t + 1):
                    acc = acc + vs[s] * w8_ref[g * 64 + t * 8 + s]
                acc = acc + b8_ref[g * 8 + t]
                ya_ref[t * SB:(t + 1) * SB, cols] = (u_ref[t * SB:(t + 1) * SB, cols].astype(f32) * acc).astype(bf16)

    o_ref[...] = ga_ref[...].astype(f32) * _dot(ya_ref[...], wba_ref[...])


def _gmlp(zu, zv, zg, ws, bs, wba_bf, layer):
    bias_full = jnp.repeat(bs.T, DG_A, axis=1)
    w8 = ws[:, :DEC_SEQ, :DEC_SEQ].reshape(G_A * DEC_SEQ * DEC_SEQ)
    b8 = bs[:, :DEC_SEQ].reshape(G_A * DEC_SEQ)
    return pl.pallas_call(
        _gmlp_kernel,
        out_shape=jax.ShapeDtypeStruct((T, D), f32),
        grid=(N_RB,),
        in_specs=[
            pl.BlockSpec((TR, D_A), lambda i: (i, 0)),
            pl.BlockSpec((TR, D_A), lambda i: (i, 0)),
            pl.BlockSpec((TR, D), lambda i: (i, 0)),
            pl.BlockSpec((G_A, CHUNK, CHUNK), lambda i: (0, 0, 0)),
            pl.BlockSpec((CHUNK, D_A), lambda i: (0, 0)),
            pl.BlockSpec(memory_space=pltpu.SMEM),
            pl.BlockSpec(memory_space=pltpu.SMEM),
            pl.BlockSpec((D_A, D), lambda i: (0, 0)),
        ],
        out_specs=pl.BlockSpec((TR, D), lambda i: (i, 0)),
        scratch_shapes=[pltpu.VMEM((G_A, CHUNK, CHUNK), bf16), pltpu.VMEM((TR, D_A), bf16)],
        compiler_params=_cparams(("arbitrary",)),
        name=f"gmlp{layer}",
    )(zu, zv, zg, ws, bias_full, w8, b8, wba_bf)


def _hgrn_prompt_kernel(q_ref, g_ref, v_ref, og_ref, pst_ref, lev_ref, gain_ref, on_ref, so_ref, st_ref):
    c_len = HG_C
    st_ref[...] = jnp.zeros((DV, DK), f32)
    gain = gain_ref[...]

    def chunk(c, carry):
        r = pl.ds(pl.multiple_of(c * c_len, c_len), c_len)
        q = q_ref[r, :].astype(f32)
        g = g_ref[r, :]
        v = v_ref[r, :]
        g1 = g.astype(bf16)
        r1 = g - g1.astype(f32)
        g2 = r1.astype(bf16)
        g3 = (r1 - g2.astype(f32)).astype(bf16)
        gs = _dot(pst_ref[...], jnp.concatenate([g1, g2, g3], axis=1))
        gs = gs[:, :DK] + gs[:, DK:2 * DK] + gs[:, 2 * DK:]
        a = gs[0:c_len]
        b = gs[c_len:2 * c_len]
        kk = 1.0 - jnp.exp(g)
        st = st_ref[...]
        o = _dot_nt((q * jnp.exp(a)).astype(bf16), st.astype(bf16))
        lev = lev_ref[...]
        sc = jnp.where(lev == 0, _dot_nt(q.astype(bf16), kk.astype(bf16)), 0.0)
        for l in range(1, HG_NL + 1):
            e = jnp.exp(gs[(1 + l) * c_len:(2 + l) * c_len])
            sc = sc + jnp.where(lev == l, _dot_nt((q * e).astype(bf16), (kk * e).astype(bf16)), 0.0)
        o = o + _dot(sc.astype(bf16), v)
        st_ref[...] = jnp.exp(a[c_len - 1:c_len, :]) * st + _dot_tn(v, (kk * jnp.exp(b)).astype(bf16))
        on_ref[r, :] = (_rms(o) * gain * og_ref[r, :].astype(f32)).astype(bf16)
        return carry

    lax.fori_loop(0, SEQ // c_len, chunk, 0)
    so_ref[...] = st_ref[...].T


def _hgrn_prompt(zqo, zf, zi, gain, layer):
    spec = lambda off: pl.BlockSpec((SEQ, DK), lambda b, h: (b, off + h))
    nrow = _PST_NP.shape[0]
    return pl.pallas_call(
        _hgrn_prompt_kernel,
        out_shape=(jax.ShapeDtypeStruct((T, D), bf16), jax.ShapeDtypeStruct((BATCH, H_B, DK, DV), f32)),
        grid=(BATCH, H_B),
        in_specs=[
            spec(0), spec(0), spec(0), spec(H_B),
            pl.BlockSpec((nrow, HG_C), lambda b, h: (0, 0)),
            pl.BlockSpec((HG_C, HG_C), lambda b, h: (0, 0)),
            pl.BlockSpec((None, 1, DV), lambda b, h: (layer, 0, h)),
        ],
        out_specs=(pl.BlockSpec((SEQ, DV), lambda b, h: (b, h)),
                   pl.BlockSpec((None, None, DK, DV), lambda b, h: (b, h, 0, 0))),
        scratch_shapes=[pltpu.VMEM((DV, DK), f32)],
        compiler_params=_cparams(("arbitrary", "arbitrary")),
        name=f"hgrn_prompt{layer}",
    )(zqo, zf, zi, zqo, jnp.asarray(_PST_NP, bf16), jnp.asarray(_LEV_NP), gain.reshape(DEPTH, 1, D))


def _hgrn_sample_kernel(*refs):
    q_ref, g_ref, v_ref, og_ref, s_ref, gain_ref = refs[:6]
    on_ref, so_ref, q_scr, v_scr, og_scr, o_scr = refs[-6:]
    gain = gain_ref[...]
    tio = lax.broadcasted_iota(jnp.int32, (DEC_SEQ, DK), 0)
    q_scr[...] = q_ref[...].astype(f32)
    v_scr[...] = v_ref[...].astype(f32)
    og_scr[...] = og_ref[...].astype(f32)

    def per_batch(b, carry):
        rows = pl.ds(b, DEC_SEQ, stride=SB)
        q = q_scr[rows, :]
        g = g_ref[rows, :]
        v = v_scr[rows, :]
        a = g
        for sh in (1, 2, 4):
            a = a + jnp.where(tio >= sh, pltpu.roll(a, sh, 0), 0.0)
        kk = 1.0 - jnp.exp(g)
        s0 = s_ref[b]
        o = _dot((q * jnp.exp(a)).astype(bf16), s0.astype(bf16))
        for s in range(DEC_SEQ):
            d = jnp.where(tio >= s, a - a[s:s + 1, :], -jnp.inf)
            w = jnp.sum(jnp.exp(d) * q * kk[s:s + 1, :], axis=-1, keepdims=True)
            o = o + w * v[s:s + 1, :]
        a_last = a[DEC_SEQ - 1:DEC_SEQ, :]
        kb = (kk * jnp.exp(a_last - a)).astype(bf16)
        f_col = jnp.transpose(jnp.exp(a))[:, DEC_SEQ - 1:DEC_SEQ]
        so_ref[b] = f_col * s0 + _dot_tn(kb, v.astype(bf16))
        o_scr[rows, :] = _rms(o) * gain * og_scr[rows, :]
        return carry

    lax.fori_loop(0, SB, per_batch, 0)
    on_ref[...] = o_scr[...].astype(bf16)


def _hgrn_sample(zqo, zf, zi, on, state_hgrn, hs_prev, gain, layer):
    rb0 = T_P // TR
    spec = lambda off: pl.BlockSpec((TR, DK), lambda g, h: (rb0 + g, off + h))
    sspec = pl.BlockSpec((None, SB, None, DK, DV), lambda g, h: (layer, g, h, 0, 0))
    ins = [zqo, zf, zi, zqo, state_hgrn, gain.reshape(DEPTH, 1, D), on]
    in_specs = [spec(0), spec(0), spec(0), spec(H_B), sspec,
                pl.BlockSpec((None, 1, DV), lambda g, h: (layer, 0, h)),
                pl.BlockSpec(memory_space=pl.ANY)]
    aliases = {6: 0}
    if hs_prev is not None:
        ins.append(hs_prev)
        in_specs.append(pl.BlockSpec(memory_space=pl.ANY))
        aliases[7] = 1
    return pl.pallas_call(
        _hgrn_sample_kernel,
        out_shape=(jax.ShapeDtypeStruct((T, D), bf16), jax.ShapeDtypeStruct(state_hgrn.shape, f32)),
        grid=(N_SG, H_B),
        in_specs=in_specs,
        out_specs=(pl.BlockSpec((TR, DV), lambda g, h: (rb0 + g, h)), sspec),
        scratch_shapes=[pltpu.VMEM((TR, DV), f32)] * 4,
        input_output_aliases=aliases,
        compiler_params=_cparams(("arbitrary", "arbitrary")),
        name=f"hgrn_sample{layer}",
    )(*ins)


def _mix_kernel(on_ref, w_ref, a_ref, gb_ref, o_ref, wbf_ref):
    @pl.when(pl.program_id(1) == 0)
    def _():
        wbf_ref[...] = w_ref[...].astype(bf16)

    o_ref[...] = (a_ref[...] + gb_ref[...].astype(f32) * _dot(on_ref[...], wbf_ref[...])).astype(bf16)


def _branch_mix(on, w_bb, a_gated, zg, layer):
    return pl.pallas_call(
        _mix_kernel,
        out_shape=jax.ShapeDtypeStruct((T, D), bf16),
        grid=(D // TN, N_MB),
        in_specs=[
            pl.BlockSpec((TM, D), lambda n, m: (m, 0)),
            pl.BlockSpec((None, D, TN), lambda n, m: (layer, 0, n)),
            pl.BlockSpec((TM, TN), lambda n, m: (m, n)),
            pl.BlockSpec((TM, TN), lambda n, m: (m, D // TN + n)),
        ],
        out_specs=pl.BlockSpec((TM, TN), lambda n, m: (m, n)),
        scratch_shapes=[pltpu.VMEM((D, TN), bf16)],
        compiler_params=_cparams(("arbitrary", "arbitrary")),
        name=f"branch_mix{layer}",
    )(on, w_bb, a_gated, zg)


def _outproj_kernel(y_ref, w_ref, x_ref, n2_ref, g1_ref, sh_ref, sc_ref, xo_ref, h_ref):
    xo_ref[...] = _dot(y_ref[...], w_ref[...])
    n2 = n2_ref[...]
    g1 = g1_ref[...]
    sh = sh_ref[...]
    sc1 = 1.0 + sc_ref[...]
    for s in range(TR // SB):
        rows = slice(s * SB, (s + 1) * SB)
        xm = x_ref[rows, :] + g1 * xo_ref[rows, :]
        xo_ref[rows, :] = xm
        h_ref[rows, :] = (_rms(xm) * n2 * sc1 + sh).astype(bf16)


def _outproj(y, w_out_bf, x, n2, mods, layer):
    return pl.pallas_call(
        _outproj_kernel,
        out_shape=(jax.ShapeDtypeStruct((T, D), f32), jax.ShapeDtypeStruct((T, D), bf16)),
        grid=(N_RB,),
        in_specs=[
            pl.BlockSpec((TR, D), lambda i: (i, 0)),
            pl.BlockSpec((None, D, D), lambda i: (layer, 0, 0), pipeline_mode=pl.Buffered(1)),
            pl.BlockSpec((TR, D), lambda i: (i, 0)),
            pl.BlockSpec((None, 1, D), lambda i: (layer, 0, 0)),
            _mod_spec(layer, _MOD_G1), _mod_spec(layer, _MOD_SH2), _mod_spec(layer, _MOD_SC2),
        ],
        out_specs=(pl.BlockSpec((TR, D), lambda i: (i, 0)), pl.BlockSpec((TR, D), lambda i: (i, 0))),
        compiler_params=_cparams(("arbitrary",)),
        name=f"outproj{layer}",
    )(y, w_out_bf, x, n2.reshape(DEPTH, 1, D), mods, mods, mods)


def _conv3(cur, m1, m2, cw_ref, cb_ref):
    return cb_ref[...] + cw_ref[0:1, :] * m2 + cw_ref[1:2, :] * m1 + cw_ref[2:3, :] * cur


def _up_kernel(h_ref, wa_ref, wb_ref, cwa_ref, cwb_ref, cba_ref, cbb_ref, sa_ref, sb_ref,
               act_ref, ta_ref, tb_ref, wabf_ref, wbbf_ref, haloa_ref, halob_ref):
    m = pl.program_id(1)

    @pl.when(m == 0)
    def _():
        wabf_ref[...] = wa_ref[...].astype(bf16)
        wbbf_ref[...] = wb_ref[...].astype(bf16)
        ta_ref[...] = jnp.zeros_like(ta_ref)
        tb_ref[...] = jnp.zeros_like(tb_ref)
        haloa_ref[...] = jnp.zeros_like(haloa_ref)
        halob_ref[...] = jnp.zeros_like(halob_ref)

    h = h_ref[...]
    acc_a = _dot(h, wabf_ref[...])
    acc_b = _dot(h, wbbf_ref[...])
    seq_start = (m % MB_PER_SEQ) == 0
    r8 = lax.broadcasted_iota(jnp.int32, (8, TC_FF), 0)

    def conv(cur, halo_ref, cw_ref, cb_ref):
        body = _conv3(cur, pltpu.roll(cur, 1, 0), pltpu.roll(cur, 2, 0), cw_ref, cb_ref)
        halo = jnp.where(seq_start, 0.0, halo_ref[...])
        cur8 = cur[0:8, :]
        m1 = jnp.where(r8 < 1, pltpu.roll(halo, 1, 0), pltpu.roll(cur8, 1, 0))
        m2 = jnp.where(r8 < 2, pltpu.roll(halo, 2, 0), pltpu.roll(cur8, 2, 0))
        return body, _conv3(cur8, m1, m2, cw_ref, cb_ref)

    ca, ca8 = conv(acc_a, haloa_ref, cwa_ref, cba_ref)
    cb, cb8 = conv(acc_b, halob_ref, cwb_ref, cbb_ref)
    act_ref[...] = (jax.nn.gelu(ca) * cb).astype(bf16)
    act_ref[0:8, :] = (jax.nn.gelu(ca8) * cb8).astype(bf16)
    ta8 = acc_a[TM - 8:TM, :]
    tb8 = acc_b[TM - 8:TM, :]
    haloa_ref[...] = ta8
    halob_ref[...] = tb8

    @pl.when((m < N_PMB) & (m % MB_PER_SEQ == MB_PER_SEQ - 1))
    def _():
        r = pl.ds(pl.multiple_of((m // MB_PER_SEQ) * 8, 8), 8)
        ta_ref[r, :] = ta8
        tb_ref[r, :] = tb8

    @pl.when(m == N_PMB)
    def _():
        def conv_s(acc, st_ref, cw_ref, cb_ref, g, t):
            slab = lambda j: acc[(g * DEC_SEQ + j) * SB:(g * DEC_SEQ + j + 1) * SB, :]
            m1 = slab(t - 1) if t >= 1 else st_ref[1, g]
            m2 = slab(t - 2) if t >= 2 else st_ref[t, g]
            return _conv3(slab(t), m1, m2, cw_ref, cb_ref)

        for g in range(N_SG):
            for t in range(DEC_SEQ):
                ca_ = conv_s(acc_a, sa_ref, cwa_ref, cba_ref, g, t)
                cb_ = conv_s(acc_b, sb_ref, cwb_ref, cbb_ref, g, t)
                r0 = (g * DEC_SEQ + t) * SB
                act_ref[r0:r0 + SB, :] = (jax.nn.gelu(ca_) * cb_).astype(bf16)
            for j in range(CONV_W - 1):
                r0 = (g * DEC_SEQ + DEC_SEQ - (CONV_W - 1) + j) * SB
                d0 = 8 * 16 + (g * (CONV_W - 1) + j) * SB
                ta_ref[d0:d0 + SB, :] = acc_a[r0:r0 + SB, :]
                tb_ref[d0:d0 + SB, :] = acc_b[r0:r0 + SB, :]


def _up_act(h, w_up, conv_w, conv_b, conv_state, layer):
    nb = N_CFF
    cws = lambda off: pl.BlockSpec((None, CONV_W, TC_FF), lambda n, m: (layer, 0, off + n))
    cbs = lambda off: pl.BlockSpec((None, 1, TC_FF), lambda n, m: (layer, 0, off + n))
    sts = lambda off: pl.BlockSpec((CONV_W - 1, N_SG, SB, TC_FF), lambda n, m: (0, 0, 0, off + n))
    ws = lambda off: pl.BlockSpec((None, D, TC_FF), lambda n, m: (layer, 0, off + n))
    cb3 = conv_b.reshape(DEPTH, 1, 2 * D_FF)
    return pl.pallas_call(
        _up_kernel,
        out_shape=(jax.ShapeDtypeStruct((T, D_FF), bf16), jax.ShapeDtypeStruct((TAIL_ROWS, D_FF), f32),
                   jax.ShapeDtypeStruct((TAIL_ROWS, D_FF), f32)),
        grid=(nb, N_MB),
        in_specs=[pl.BlockSpec((TM, D), lambda n, m: (m, 0)), ws(0), ws(nb), cws(0), cws(nb), cbs(0), cbs(nb),
                  sts(0), sts(nb)],
        out_specs=(pl.BlockSpec((TM, TC_FF), lambda n, m: (m, n)),
                   pl.BlockSpec((TAIL_ROWS, TC_FF), lambda n, m: (0, n)),
                   pl.BlockSpec((TAIL_ROWS, TC_FF), lambda n, m: (0, n))),
        scratch_shapes=[pltpu.VMEM((D, TC_FF), bf16), pltpu.VMEM((D, TC_FF), bf16),
                        pltpu.VMEM((8, TC_FF), f32), pltpu.VMEM((8, TC_FF), f32)],
        compiler_params=_cparams(("arbitrary", "arbitrary")),
        name=f"up_act{layer}",
    )(h, w_up, w_up, conv_w, conv_w, cb3, cb3, conv_state, conv_state)


def _ffn_kernel(act_ref, wd_ref, xm_ref, g2_ref, gain_ref, sh_ref, sc_ref, xo_ref, ho_ref, *, final):
    xo_ref[...] = _dot(act_ref[...], wd_ref[...])
    g2 = g2_ref[...]
    gain = gain_ref[...]
    for s in range(TR_FF // SB):
        rows = slice(s * SB, (s + 1) * SB)
        xn = xm_ref[rows, :] + g2 * xo_ref[rows, :]
        xo_ref[rows, :] = xn
        if final:
            ho_ref[rows, :] = _rms(xn) * gain
        else:
            ho_ref[rows, :] = (_rms(xn) * gain * (1.0 + sc_ref[...]) + sh_ref[...]).astype(bf16)


def _ffn(act, wd_bf, x_mid, mods, gain, layer, final):
    nxt = layer if final else layer + 1
    row = lambda i: (i, 0)
    return pl.pallas_call(
        functools.partial(_ffn_kernel, final=final),
        out_shape=(jax.ShapeDtypeStruct((T, D), f32), jax.ShapeDtypeStruct((T, D), f32 if final else bf16)),
        grid=(T // TR_FF,),
        in_specs=[
            pl.BlockSpec((TR_FF, D_FF), row),
            pl.BlockSpec((None, D_FF, D), lambda i: (layer, 0, 0), pipeline_mode=pl.Buffered(1)),
            pl.BlockSpec((TR_FF, D), row),
            _mod_spec(layer, _MOD_G2, TR_FF),
            pl.BlockSpec((1, D), lambda i: (0, 0)),
            _mod_spec(nxt, _MOD_SH1, TR_FF), _mod_spec(nxt, _MOD_SC1, TR_FF),
        ],
        out_specs=(pl.BlockSpec((TR_FF, D), row), pl.BlockSpec((TR_FF, D), row)),
        compiler_params=_cparams(("arbitrary",)),
        name=f"ffn{layer}",
    )(act, wd_bf, x_mid, mods, gain.reshape(1, D), mods, mods)


def _to_rows(a):
    f = a.shape[-1]
    return a.reshape(N_SG, SB, DEC_SEQ, f).transpose(0, 2, 1, 3).reshape(T_S, f)


def _from_rows(a, steps=DEC_SEQ):
    f = a.shape[-1]
    return a.reshape(N_SG, steps, SB, f).transpose(0, 2, 1, 3).reshape(DEC_BATCH, steps, f)


def kernel(x_prompt, x_sample, c_prompt, c_sample, state_hgrn, state_ffn_conv, ada_w, ada_b, norm1_g, norm2_g,
           w_in, gmlp_v_g, gmlp_ws, gmlp_bs, w_branch_a, hgrn_lower_bounds, hgrn_norm_g, w_branch_b, w_out,
           w_up, conv_w, conv_b, w_down, final_norm_g):
    x = jnp.concatenate([x_prompt.reshape(T_P, D), _to_rows(x_sample)], axis=0)
    c_all = jnp.concatenate([c_sample, c_prompt, jnp.zeros((4, D), f32)], axis=0)
    mods = _ada(c_all, ada_w, ada_b)

    wba_bf = w_branch_a.astype(bf16)
    wout_bf = w_out.astype(bf16)
    wd_bf = w_down.astype(bf16)
    conv_st = state_ffn_conv.transpose(0, 2, 1, 3).reshape(DEPTH, CONV_W - 1, N_SG, SB, 2 * D_FF)

    h = _norm_mod(x, norm1_g[0], mods, 0)
    hs_all = None
    hp, cp, cs, vp, vs = [], [], [], [], []
    y = None
    for l in range(DEPTH):
        zu, zv, zqo, zf, zi, zg = _inproj(h, w_in, hgrn_lower_bounds, gmlp_v_g, l)
        a_gated = _gmlp(zu, zv, zg, gmlp_ws[l], gmlp_bs[l], wba_bf[l], l)
        on, s_p = _hgrn_prompt(zqo, zf, zi, hgrn_norm_g, l)
        on, hs_all = _hgrn_sample(zqo, zf, zi, on, state_hgrn, hs_all, hgrn_norm_g, l)
        ymix = _branch_mix(on, w_branch_b, a_gated, zg, l)
        x_mid, h2 = _outproj(ymix, wout_bf, x, norm2_g, mods, l)
        act, tail_a, tail_b = _up_act(h2, w_up, conv_w, conv_b, conv_st[l], l)
        final = l == DEPTH - 1
        gain = final_norm_g if final else norm1_g[l + 1]
        x, hn = _ffn(act, wd_bf, x_mid, mods, gain, l, final)
        if final:
            y = hn
        else:
            h = hn
        tail = jnp.concatenate([tail_a, tail_b], axis=1)
        hp.append(s_p)
        cp.append(tail[:8 * BATCH].reshape(BATCH, 8, 2 * D_FF)[:, 8 - (CONV_W - 1):])
        cs.append(_from_rows(tail[8 * 16:], CONV_W - 1))
        vp.append(jnp.stack([zv[(b + 1) * SEQ - CHUNK:(b + 1) * SEQ] for b in range(BATCH)]))
        vs.append(_from_rows(zv[T_P:]))

    y_prompt = y[:T_P].reshape(BATCH, SEQ, D)
    y_sample = _from_rows(y[T_P:])
    return (y_prompt, y_sample, jnp.stack(hp), hs_all, jnp.stack(cp), jnp.stack(cs), jnp.stack(vp), jnp.stack(vs))
```

```python
import functools

import numpy as np
import jax
import jax.numpy as jnp
from jax import lax
from jax.experimental import pallas as pl
from jax.experimental.pallas import tpu as pltpu

f32 = jnp.float32
bf16 = jnp.bfloat16

D = 2048
BATCH = 4
SEQ = 2048
DEPTH = 2
DEC_BATCH = 128
DEC_SEQ = 8
G_A = 8
DG_A = 128
D_A = G_A * DG_A
CHUNK = 128
H_B = 16
DK = 128
DV = 128
D_FF = 5632
CONV_W = 3
N_IN = 2 * D_A + 6 * D
EPS = 1e-6

T_P = BATCH * SEQ
T_S = DEC_BATCH * DEC_SEQ
T = T_P + T_S
SB = 64
N_SG = DEC_BATCH // SB
TR = DEC_SEQ * SB
N_RB = T // TR
N_PRB = T_P // TR
RB_PER_SEQ = SEQ // TR
TM = 1024
N_MB = T // TM
N_PMB = T_P // TM
MB_PER_SEQ = SEQ // TM
TN = 1024
TC_FF = 512
N_CFF = D_FF // TC_FF
TR_FF = 256
HG_C = 128
HG_NH = 4
HG_BLOCK_LEVELS = (64, 32, 16)
HG_MASK_LEVELS = (8, 4, 2, 1)
HS_UNROLL = 16
LOG2E = 1.4426950408889634
TAIL_ROWS = 8 * 16 + (CONV_W - 1) * DEC_BATCH

_U0, _V0, _Q0, _F0, _I0, _OG0, _GA0 = 0, 1, 2, 4, 6, 8, 10
_MOD_SH1, _MOD_SC1, _MOD_G1, _MOD_SH2, _MOD_SC2, _MOD_G2 = range(6)
N_MODJ = BATCH + N_SG


def _cparams(sem):
    return pltpu.CompilerParams(dimension_semantics=sem)


def _hgrn_level_table(c):
    t = np.arange(c)
    lev = np.full((c, c), -1, np.int32)
    lev[t, t] = 0
    for h in HG_MASK_LEVELS:
        upper = (t % (2 * h)) >= h
        same = (t[:, None] // (2 * h)) == (t[None, :] // (2 * h))
        lev[same & upper[:, None] & ~upper[None, :]] = h
    return lev


_LEV_NP = _hgrn_level_table(HG_C)


def _rms(x, eps=EPS):
    return x * lax.rsqrt(jnp.mean(x * x, axis=-1, keepdims=True) + eps)


def _dot(a, b):
    return jnp.dot(a, b, preferred_element_type=f32)


def _dot_nt(a, b):
    return lax.dot_general(a, b, (((1,), (1,)), ((), ())), preferred_element_type=f32)


def _dot_tn(a, b):
    return lax.dot_general(a, b, (((0,), (0,)), ((), ())), preferred_element_type=f32)


def _modj(i, rows):
    return jnp.where(i < T_P // rows, i // (SEQ // rows), BATCH + (i - T_P // rows) // (TR // rows))


def _mod_spec(layer, kind, rows=TR):
    return pl.BlockSpec((None, None, None, SB, D), lambda i, *_: (layer, kind, _modj(i, rows), 0, 0))


def _ada_kernel(c_ref, w_ref, b_ref, o_ref):
    c = c_ref[...]
    a = (c * jax.nn.sigmoid(c)).astype(bf16)
    r = _dot(a, w_ref[...].astype(bf16)) + b_ref[...]
    for g in range(N_SG):
        o_ref[BATCH + g] = r[g * SB:(g + 1) * SB]
    for j in range(BATCH):
        o_ref[j] = jnp.broadcast_to(r[DEC_BATCH + j:DEC_BATCH + j + 1], (SB, TN))


def _ada(c_all, ada_w, ada_b):
    nrow = c_all.shape[0]
    ncol = 6 * D // TN
    per_kind = D // TN
    return pl.pallas_call(
        _ada_kernel,
        out_shape=jax.ShapeDtypeStruct((DEPTH, 6, N_MODJ, SB, D), f32),
        grid=(DEPTH, ncol),
        in_specs=[
            pl.BlockSpec((nrow, D), lambda l, n: (0, 0)),
            pl.BlockSpec((None, D, TN), lambda l, n: (l, 0, n)),
            pl.BlockSpec((None, 1, TN), lambda l, n: (l, 0, n)),
        ],
        out_specs=pl.BlockSpec((None, None, N_MODJ, SB, TN), lambda l, n: (l, n // per_kind, 0, 0, n % per_kind)),
        compiler_params=_cparams(("arbitrary", "arbitrary")),
        name="ada_mod",
    )(c_all, ada_w, ada_b.reshape(DEPTH, 1, 6 * D))


def _norm_mod_kernel(x_ref, g_ref, sh_ref, sc_ref, o_ref):
    g = g_ref[...]
    sh = sh_ref[...]
    sc1 = 1.0 + sc_ref[...]
    for s in range(TR // SB):
        rows = slice(s * SB, (s + 1) * SB)
        o_ref[rows, :] = (_rms(x_ref[rows, :]) * g * sc1 + sh).astype(bf16)


def _norm_mod(x, gain, mods, layer):
    return pl.pallas_call(
        _norm_mod_kernel,
        out_shape=jax.ShapeDtypeStruct((T, D), bf16),
        grid=(N_RB,),
        in_specs=[
            pl.BlockSpec((TR, D), lambda i: (i, 0)),
            pl.BlockSpec((1, D), lambda i: (0, 0)),
            _mod_spec(layer, _MOD_SH1),
            _mod_spec(layer, _MOD_SC1),
        ],
        out_specs=pl.BlockSpec((TR, D), lambda i: (i, 0)),
        compiler_params=_cparams(("arbitrary",)),
        name="norm_mod",
    )(x, gain.reshape(1, D), mods, mods)


def _proj_kernel(h_ref, w_ref, *rest, epilogue):
    *extra, o_ref, wbf_ref = rest

    @pl.when(pl.program_id(1) == 0)
    def _():
        wbf_ref[...] = w_ref[...].astype(bf16)

    epilogue(_dot(h_ref[...], wbf_ref[...]), o_ref, *extra)


def _epi_gelu(acc, o_ref):
    o_ref[...] = jax.nn.gelu(acc).astype(o_ref.dtype)


def _epi_v(acc, o_ref, vg_ref):
    v = jax.nn.gelu(acc)
    for g in range(G_A):
        cols = slice(g * DG_A, (g + 1) * DG_A)
        o_ref[:, cols] = _rms(v[:, cols]) * vg_ref[:, cols]


def _epi_silu(acc, o_ref, scale_ref):
    o_ref[...] = (acc * jax.nn.sigmoid(acc) * scale_ref[pl.program_id(0)]).astype(o_ref.dtype)


def _epi_logf(acc, o_ref, lb_ref, *, layer):
    p = jax.nn.softmax(lb_ref[...], axis=0)
    lb = jnp.zeros((1, TN), f32)
    for j in range(1, layer + 1):
        lb = lb + p[j:j + 1]
    o_ref[...] = jnp.logaddexp(jnp.log(lb), jnp.log1p(-lb) + jax.nn.log_sigmoid(acc))


def _epi_id(acc, o_ref):
    o_ref[...] = acc.astype(o_ref.dtype)


def _epi_sigmoid(acc, o_ref):
    o_ref[...] = jax.nn.sigmoid(acc).astype(o_ref.dtype)


def _proj(name, h, w_in, layer, col_of, n_tiles, out_dtype, epilogue, extra=(), extra_specs=()):
    return pl.pallas_call(
        functools.partial(_proj_kernel, epilogue=epilogue),
        out_shape=jax.ShapeDtypeStruct((T, n_tiles * TN), out_dtype),
        grid=(n_tiles, N_MB),
        in_specs=[
            pl.BlockSpec((TM, D), lambda n, m: (m, 0)),
            pl.BlockSpec((None, D, TN), lambda n, m: (layer, 0, col_of(n))),
            *extra_specs,
        ],
        out_specs=pl.BlockSpec((TM, TN), lambda n, m: (m, n)),
        scratch_shapes=[pltpu.VMEM((D, TN), bf16)],
        compiler_params=_cparams(("arbitrary", "arbitrary")),
        name=f"{name}{layer}",
    )(h, w_in, *extra)


def _inproj(h, w_in, lower_bounds, v_gain, layer):
    smem = pl.BlockSpec(memory_space=pltpu.SMEM)
    zu = _proj("in_u", h, w_in, layer, lambda n: _U0 + n, 1, bf16, _epi_gelu)
    zv = _proj("in_v", h, w_in, layer, lambda n: _V0 + n, 1, f32, _epi_v,
               (v_gain.reshape(DEPTH, 1, D_A),), (pl.BlockSpec((None, 1, D_A), lambda n, m: (layer, 0, 0)),))
    scales = jnp.asarray([DK ** -0.5] * 2 + [1.0] * 2, f32)
    zqo = _proj("in_qo", h, w_in, layer, lambda n: jnp.where(n < 2, _Q0 + n, _OG0 + n - 2), 4, bf16, _epi_silu,
                (scales,), (smem,))
    zf = _proj("in_f", h, w_in, layer, lambda n: _F0 + n, 2, f32, functools.partial(_epi_logf, layer=layer),
               (lower_bounds,), (pl.BlockSpec((DEPTH, TN), lambda n, m: (0, n)),))
    zi = _proj("in_i", h, w_in, layer, lambda n: _I0 + n, 2, bf16, _epi_id)
    zg = _proj("in_g", h, w_in, layer, lambda n: _GA0 + n, 4, bf16, _epi_sigmoid)
    return zu, zv, zqo, zf, zi, zg


def _gmlp_kernel(u_ref, v_ref, ga_ref, ws_ref, bias_ref, w8_ref, b8_ref, wba_ref, o_ref, wm_ref, ya_ref):
    i = pl.program_id(0)

    @pl.when(i == 0)
    def _():
        r = lax.broadcasted_iota(jnp.int32, (CHUNK, CHUNK), 0)
        c = lax.broadcasted_iota(jnp.int32, (CHUNK, CHUNK), 1)
        for g in range(G_A):
            wm_ref[g] = jnp.where(r >= c, ws_ref[g], 0.0).astype(bf16)

    @pl.when(i < N_PRB)
    def _():
        for ch in range(TR // CHUNK):
            rows = slice(ch * CHUNK, (ch + 1) * CHUNK)
            for g in range(G_A):
                cols = slice(g * DG_A, (g + 1) * DG_A)
                s = _dot(wm_ref[g], v_ref[rows, cols].astype(bf16)) + bias_ref[:, cols]
                ya_ref[rows, cols] = (u_ref[rows, cols].astype(f32) * s).astype(bf16)

    @pl.when(i >= N_PRB)
    def _():
        for g in range(G_A):
            cols = slice(g * DG_A, (g + 1) * DG_A)
            vs = [v_ref[s * SB:(s + 1) * SB, cols] for s in range(DEC_SEQ)]
            for t in range(DEC_SEQ):
                acc = vs[0] * w8_ref[g * 64 + t * 8]
                for s in range(1, t + 1):
                    acc = acc + vs[s] * w8_ref[g * 64 + t * 8 + s]
                acc = acc + b8_ref[g * 8 + t]
                ya_ref[t * SB:(t + 1) * SB, cols] = (u_ref[t * SB:(t + 1) * SB, cols].astype(f32) * acc).astype(bf16)

    o_ref[...] = ga_ref[...].astype(f32) * _dot(ya_ref[...], wba_ref[...])


def _gmlp(zu, zv, zg, ws, bs, wba_bf, layer):
    bias_full = jnp.repeat(bs.T, DG_A, axis=1)
    w8 = ws[:, :DEC_SEQ, :DEC_SEQ].reshape(G_A * DEC_SEQ * DEC_SEQ)
    b8 = bs[:, :DEC_SEQ].reshape(G_A * DEC_SEQ)
    return pl.pallas_call(
        _gmlp_kernel,
        out_shape=jax.ShapeDtypeStruct((T, D), f32),
        grid=(N_RB,),
        in_specs=[
            pl.BlockSpec((TR, D_A), lambda i: (i, 0)),
            pl.BlockSpec((TR, D_A), lambda i: (i, 0)),
            pl.BlockSpec((TR, D), lambda i: (i, 0)),
            pl.BlockSpec((G_A, CHUNK, CHUNK), lambda i: (0, 0, 0)),
            pl.BlockSpec((CHUNK, D_A), lambda i: (0, 0)),
            pl.BlockSpec(memory_space=pltpu.SMEM),
            pl.BlockSpec(memory_space=pltpu.SMEM),
            pl.BlockSpec((D_A, D), lambda i: (0, 0)),
        ],
        out_specs=pl.BlockSpec((TR, D), lambda i: (i, 0)),
        scratch_shapes=[pltpu.VMEM((G_A, CHUNK, CHUNK), bf16), pltpu.VMEM((TR, D_A), bf16)],
        compiler_params=_cparams(("arbitrary",)),
        name=f"gmlp{layer}",
    )(zu, zv, zg, ws, bias_full, w8, b8, wba_bf)


def _bcast_row(x8, r):
    return jnp.broadcast_to(x8[r:r + 1, :], x8.shape)


def _cat_rows(tiles):
    return tiles[0] if len(tiles) == 1 else jnp.concatenate(tiles, axis=0)


def _hgrn_chunk(q, g, v, sts, lev, r8, nh):
    c = HG_C
    nv = c // 8
    w = nh * DK
    hs = [slice(i * DK, (i + 1) * DK) for i in range(nh)]
    gl = g * LOG2E
    x_t = []
    for j in range(nv):
        x = gl[8 * j:8 * j + 8, :]
        for sh in (1, 2, 4):
            x = x + jnp.where(r8 >= sh, pltpu.roll(x, sh, 0), 0.0)
        x_t.append(x)
    last = [_bcast_row(x, 7) for x in x_t]
    sh = 1
    while sh < nv:
        last = [last[j] + last[j - sh] if j >= sh else last[j] for j in range(nv)]
        sh *= 2
    a_t = [x_t[0]] + [x_t[j] + last[j - 1] for j in range(1, nv)]
    f = jnp.exp2(gl)
    kk = 1.0 - f
    q_t = [q[8 * j:8 * j + 8, :] for j in range(nv)]
    k_t = [kk[8 * j:8 * j + 8, :] for j in range(nv)]
    a = _cat_rows(a_t)
    odd = (lax.broadcasted_iota(jnp.int32, (c, w), 0) & 1) == 1

    qb = q.astype(bf16)
    kb0 = kk.astype(bf16)
    sc = [jnp.where(lev == 0, _dot_nt(qb[:, hh], kb0[:, hh]), 0.0) for hh in hs]
    for h in HG_MASK_LEVELS:
        if h == 8:
            e = _cat_rows([(a_t[j] - last[j - 1]) if j % 2 else (last[j] - a_t[j]) for j in range(nv)])
        elif h == 4:
            d = _cat_rows([t - _bcast_row(t, 3) for t in a_t])
            e = jnp.minimum(d, -d)
        elif h == 2:
            d = _cat_rows([t - jnp.where(r8 < 4, _bcast_row(t, 1), _bcast_row(t, 5)) for t in a_t])
            e = jnp.minimum(d, -d)
        if h >= 2:
            e = jnp.exp2(e)
        else:
            e = jnp.where(odd, f, 1.0)
        qe = (q * e).astype(bf16)
        ke = (kk * e).astype(bf16)
        sc = [jnp.where(lev == h, _dot_nt(qe[:, hh], ke[:, hh]), sc[i]) for i, hh in enumerate(hs)]
    sc_t = [[s_[8 * j:8 * j + 8, :] for j in range(nv)] for s_ in sc]
    zero8 = jnp.zeros((8, w), f32)
    for h in HG_BLOCK_LEVELS:
        ht = h // 8
        for blk in range(c // (2 * h)):
            lo = range(blk * 2 * ht, blk * 2 * ht + ht)
            up = range(blk * 2 * ht + ht, (blk + 1) * 2 * ht)
            ref = last[blk * 2 * ht + ht - 1]
            qt = _cat_rows([q_t[j] * jnp.exp2(a_t[j] - ref) for j in up]).astype(bf16)
            kz = _cat_rows([k_t[j] * jnp.exp2(ref - a_t[j]) if j in lo else zero8 for j in range(nv)]).astype(bf16)
            for i, hh in enumerate(hs):
                s_ = _dot_nt(qt[:, hh], kz[:, hh])
                for r, j in enumerate(up):
                    sc_t[i][j] = sc_t[i][j] + s_[8 * r:8 * r + 8, :]
    qa = (q * jnp.exp2(a)).astype(bf16)
    a_last = last[-1]
    kb = _cat_rows([k_t[j] * jnp.exp2(a_last - a_t[j]) for j in range(nv)]).astype(bf16)
    fl = jnp.exp2(a_last[0:1, :])
    o, st_new = [], []
    for i, hh in enumerate(hs):
        o.append(_dot_nt(qa[:, hh], sts[i].astype(bf16)) + _dot(_cat_rows(sc_t[i]).astype(bf16), v[:, hh]))
        st_new.append(fl[:, hh] * sts[i] + _dot_tn(v[:, hh], kb[:, hh]))
    return o, st_new


def _hgrn_prompt_kernel(q_ref, g_ref, v_ref, og_ref, lev_ref, gain_ref, on_ref, so_ref, st_ref):
    r8 = lax.broadcasted_iota(jnp.int32, (8, HG_NH * DK), 0)
    for hh in range(HG_NH):
        st_ref[hh] = jnp.zeros((DV, DK), f32)

    def chunk(c, carry):
        r = pl.ds(pl.multiple_of(c * HG_C, HG_C), HG_C)
        o, st_new = _hgrn_chunk(q_ref[r, :].astype(f32), g_ref[r, :], v_ref[r, :],
                                [st_ref[hh] for hh in range(HG_NH)], lev_ref[...], r8, HG_NH)
        for hh in range(HG_NH):
            cols = slice(hh * DV, (hh + 1) * DV)
            on_ref[r, cols] = (_rms(o[hh]) * gain_ref[:, cols] * og_ref[r, cols].astype(f32)).astype(bf16)
            st_ref[hh] = st_new[hh]
        return carry

    lax.fori_loop(0, SEQ // HG_C, chunk, 0)
    for hh in range(HG_NH):
        so_ref[hh] = st_ref[hh].T


def _hgrn_prompt(zqo, zf, zi, gain, layer):
    ng = H_B // HG_NH
    spec = lambda off: pl.BlockSpec((SEQ, HG_NH * DK), lambda b, h: (b, off + h))
    return pl.pallas_call(
        _hgrn_prompt_kernel,
        out_shape=(jax.ShapeDtypeStruct((T, D), bf16), jax.ShapeDtypeStruct((BATCH, H_B, DK, DV), f32)),
        grid=(BATCH, ng),
        in_specs=[
            spec(0), spec(0), spec(0), spec(ng),
            pl.BlockSpec((HG_C, HG_C), lambda b, h: (0, 0)),
            pl.BlockSpec((None, 1, HG_NH * DV), lambda b, h: (layer, 0, h)),
        ],
        out_specs=(pl.BlockSpec((SEQ, HG_NH * DV), lambda b, h: (b, h)),
                   pl.BlockSpec((None, HG_NH, DK, DV), lambda b, h: (b, h, 0, 0))),
        scratch_shapes=[pltpu.VMEM((HG_NH, DV, DK), f32)],
        compiler_params=_cparams(("arbitrary", "arbitrary")),
        name=f"hgrn_prompt{layer}",
    )(zqo, zf, zi, zqo, jnp.asarray(_LEV_NP), gain.reshape(DEPTH, 1, D))


def _hgrn_sample_kernel(*refs):
    q_ref, g_ref, v_ref, og_ref, s_ref, gain_ref = refs[:6]
    on_ref, so_ref, qa_scr, kb_scr, v_scr, fs_scr, o_scr = refs[-7:]
    slab = lambda t: slice(t * SB, (t + 1) * SB)
    a, q, kk, v = [], [], [], []
    for t in range(DEC_SEQ):
        gl = g_ref[slab(t), :] * LOG2E
        a.append(gl if t == 0 else a[t - 1] + gl)
        kk.append(1.0 - jnp.exp2(gl))
        q.append(q_ref[slab(t), :].astype(f32))
        v.append(v_ref[slab(t), :].astype(f32))
    a_last = a[DEC_SEQ - 1]
    for t in range(DEC_SEQ):
        qa_scr[slab(t), :] = q[t] * jnp.exp2(a[t])
        kb_scr[slab(t), :] = kk[t] * jnp.exp2(a_last - a[t])
        v_scr[slab(t), :] = v[t]
        o = jnp.sum(q[t] * kk[t], axis=-1, keepdims=True) * v[t]
        for s in range(t):
            o = o + jnp.sum(q[t] * kk[s] * jnp.exp2(a[t] - a[s]), axis=-1, keepdims=True) * v[s]
        o_scr[slab(t), :] = o
    fl = jnp.exp2(a_last)
    f1 = fl.astype(bf16).astype(f32)
    r1 = fl - f1
    f2 = r1.astype(bf16).astype(f32)
    fs_scr[slab(0), :] = f1
    fs_scr[slab(1), :] = f2
    fs_scr[slab(2), :] = r1 - f2
    fs_scr[3 * SB:, :] = jnp.zeros((TR - 3 * SB, DK), f32)

    ones3 = (lax.broadcasted_iota(jnp.int32, (DEC_SEQ, DV), 0) < 3).astype(bf16)
    zero8 = jnp.zeros((DEC_SEQ, DV), bf16)

    def per_batch(b, carry):
        rows = pl.ds(b, DEC_SEQ, stride=SB)
        s0 = s_ref[b]
        o_scr[rows, :] = o_scr[rows, :] + _dot(qa_scr[rows, :].astype(bf16), s0.astype(bf16))
        lhs = jnp.concatenate([kb_scr[rows, :], fs_scr[rows, :]], axis=0).astype(bf16)
        v8 = v_scr[rows, :].astype(bf16)
        rhs = jnp.concatenate([jnp.concatenate([v8, zero8], axis=1), jnp.concatenate([zero8, ones3], axis=1)], axis=0)
        r = _dot_tn(lhs, rhs)
        so_ref[b] = r[:, DV:] * s0 + r[:, :DV]
        return carry

    lax.fori_loop(0, SB, per_batch, 0, unroll=HS_UNROLL)
    gain = gain_ref[...]
    for t in range(DEC_SEQ):
        on_ref[slab(t), :] = (_rms(o_scr[slab(t), :]) * gain * og_ref[slab(t), :].astype(f32)).astype(bf16)


def _hgrn_sample(zqo, zf, zi, on, state_hgrn, hs_prev, gain, layer):
    rb0 = T_P // TR
    spec = lambda off: pl.BlockSpec((TR, DK), lambda g, h: (rb0 + g, off + h))
    sspec = pl.BlockSpec((None, SB, None, DK, DV), lambda g, h: (layer, g, h, 0, 0))
    ins = [zqo, zf, zi, zqo, state_hgrn, gain.reshape(DEPTH, 1, D), on]
    in_specs = [spec(0), spec(0), spec(0), spec(H_B), sspec,
                pl.BlockSpec((None, 1, DV), lambda g, h: (layer, 0, h)),
                pl.BlockSpec(memory_space=pl.ANY)]
    aliases = {6: 0}
    if hs_prev is not None:
        ins.append(hs_prev)
        in_specs.append(pl.BlockSpec(memory_space=pl.ANY))
        aliases[7] = 1
    return pl.pallas_call(
        _hgrn_sample_kernel,
        out_shape=(jax.ShapeDtypeStruct((T, D), bf16), jax.ShapeDtypeStruct(state_hgrn.shape, f32)),
        grid=(N_SG, H_B),
        in_specs=in_specs,
        out_specs=(pl.BlockSpec((TR, DV), lambda g, h: (rb0 + g, h)), sspec),
        scratch_shapes=[pltpu.VMEM((TR, DV), f32)] * 5,
        input_output_aliases=aliases,
        compiler_params=_cparams(("arbitrary", "arbitrary")),
        name=f"hgrn_sample{layer}",
    )(*ins)


def _mix_kernel(on_ref, w_ref, a_ref, gb_ref, o_ref, wbf_ref):
    @pl.when(pl.program_id(1) == 0)
    def _():
        wbf_ref[...] = w_ref[...].astype(bf16)

    o_ref[...] = (a_ref[...] + gb_ref[...].astype(f32) * _dot(on_ref[...], wbf_ref[...])).astype(bf16)


def _branch_mix(on, w_bb, a_gated, zg, layer):
    return pl.pallas_call(
        _mix_kernel,
        out_shape=jax.ShapeDtypeStruct((T, D), bf16),
        grid=(D // TN, N_MB),
        in_specs=[
            pl.BlockSpec((TM, D), lambda n, m: (m, 0)),
            pl.BlockSpec((None, D, TN), lambda n, m: (layer, 0, n)),
            pl.BlockSpec((TM, TN), lambda n, m: (m, n)),
            pl.BlockSpec((TM, TN), lambda n, m: (m, D // TN + n)),
        ],
        out_specs=pl.BlockSpec((TM, TN), lambda n, m: (m, n)),
        scratch_shapes=[pltpu.VMEM((D, TN), bf16)],
        compiler_params=_cparams(("arbitrary", "arbitrary")),
        name=f"branch_mix{layer}",
    )(on, w_bb, a_gated, zg)


def _outproj_kernel(y_ref, w_ref, x_ref, n2_ref, g1_ref, sh_ref, sc_ref, xo_ref, h_ref):
    xo_ref[...] = _dot(y_ref[...], w_ref[...])
    n2 = n2_ref[...]
    g1 = g1_ref[...]
    sh = sh_ref[...]
    sc1 = 1.0 + sc_ref[...]
    for s in range(TR // SB):
        rows = slice(s * SB, (s + 1) * SB)
        xm = x_ref[rows, :] + g1 * xo_ref[rows, :]
        xo_ref[rows, :] = xm
        h_ref[rows, :] = (_rms(xm) * n2 * sc1 + sh).astype(bf16)


def _outproj(y, w_out_bf, x, n2, mods, layer):
    return pl.pallas_call(
        _outproj_kernel,
        out_shape=(jax.ShapeDtypeStruct((T, D), f32), jax.ShapeDtypeStruct((T, D), bf16)),
        grid=(N_RB,),
        in_specs=[
            pl.BlockSpec((TR, D), lambda i: (i, 0)),
            pl.BlockSpec((None, D, D), lambda i: (layer, 0, 0), pipeline_mode=pl.Buffered(1)),
            pl.BlockSpec((TR, D), lambda i: (i, 0)),
            pl.BlockSpec((None, 1, D), lambda i: (layer, 0, 0)),
            _mod_spec(layer, _MOD_G1), _mod_spec(layer, _MOD_SH2), _mod_spec(layer, _MOD_SC2),
        ],
        out_specs=(pl.BlockSpec((TR, D), lambda i: (i, 0)), pl.BlockSpec((TR, D), lambda i: (i, 0))),
        compiler_params=_cparams(("arbitrary",)),
        name=f"outproj{layer}",
    )(y, w_out_bf, x, n2.reshape(DEPTH, 1, D), mods, mods, mods)


def _conv3(cur, m1, m2, cw_ref, cb_ref):
    return cb_ref[...] + cw_ref[0:1, :] * m2 + cw_ref[1:2, :] * m1 + cw_ref[2:3, :] * cur


def _up_kernel(h_ref, wa_ref, wb_ref, cwa_ref, cwb_ref, cba_ref, cbb_ref, sa_ref, sb_ref,
               act_ref, ta_ref, tb_ref, wabf_ref, wbbf_ref, haloa_ref, halob_ref):
    m = pl.program_id(1)

    @pl.when(m == 0)
    def _():
        wabf_ref[...] = wa_ref[...].astype(bf16)
        wbbf_ref[...] = wb_ref[...].astype(bf16)
        ta_ref[...] = jnp.zeros_like(ta_ref)
        tb_ref[...] = jnp.zeros_like(tb_ref)
        haloa_ref[...] = jnp.zeros_like(haloa_ref)
        halob_ref[...] = jnp.zeros_like(halob_ref)

    h = h_ref[...]
    acc_a = _dot(h, wabf_ref[...])
    acc_b = _dot(h, wbbf_ref[...])
    seq_start = (m % MB_PER_SEQ) == 0
    r8 = lax.broadcasted_iota(jnp.int32, (8, TC_FF), 0)

    def conv(cur, halo_ref, cw_ref, cb_ref):
        body = _conv3(cur, pltpu.roll(cur, 1, 0), pltpu.roll(cur, 2, 0), cw_ref, cb_ref)
        halo = jnp.where(seq_start, 0.0, halo_ref[...])
        cur8 = cur[0:8, :]
        m1 = jnp.where(r8 < 1, pltpu.roll(halo, 1, 0), pltpu.roll(cur8, 1, 0))
        m2 = jnp.where(r8 < 2, pltpu.roll(halo, 2, 0), pltpu.roll(cur8, 2, 0))
        return body, _conv3(cur8, m1, m2, cw_ref, cb_ref)

    ca, ca8 = conv(acc_a, haloa_ref, cwa_ref, cba_ref)
    cb, cb8 = conv(acc_b, halob_ref, cwb_ref, cbb_ref)
    act_ref[...] = (jax.nn.gelu(ca) * cb).astype(bf16)
    act_ref[0:8, :] = (jax.nn.gelu(ca8) * cb8).astype(bf16)
    ta8 = acc_a[TM - 8:TM, :]
    tb8 = acc_b[TM - 8:TM, :]
    haloa_ref[...] = ta8
    halob_ref[...] = tb8

    @pl.when((m < N_PMB) & (m % MB_PER_SEQ == MB_PER_SEQ - 1))
    def _():
        r = pl.ds(pl.multiple_of((m // MB_PER_SEQ) * 8, 8), 8)
        ta_ref[r, :] = ta8
        tb_ref[r, :] = tb8

    @pl.when(m == N_PMB)
    def _():
        def conv_s(acc, st_ref, cw_ref, cb_ref, g, t):
            slab = lambda j: acc[(g * DEC_SEQ + j) * SB:(g * DEC_SEQ + j + 1) * SB, :]
            m1 = slab(t - 1) if t >= 1 else st_ref[1, g]
            m2 = slab(t - 2) if t >= 2 else st_ref[t, g]
            return _conv3(slab(t), m1, m2, cw_ref, cb_ref)

        for g in range(N_SG):
            for t in range(DEC_SEQ):
                ca_ = conv_s(acc_a, sa_ref, cwa_ref, cba_ref, g, t)
                cb_ = conv_s(acc_b, sb_ref, cwb_ref, cbb_ref, g, t)
                r0 = (g * DEC_SEQ + t) * SB
                act_ref[r0:r0 + SB, :] = (jax.nn.gelu(ca_) * cb_).astype(bf16)
            for j in range(CONV_W - 1):
                r0 = (g * DEC_SEQ + DEC_SEQ - (CONV_W - 1) + j) * SB
                d0 = 8 * 16 + (g * (CONV_W - 1) + j) * SB
                ta_ref[d0:d0 + SB, :] = acc_a[r0:r0 + SB, :]
                tb_ref[d0:d0 + SB, :] = acc_b[r0:r0 + SB, :]


def _up_act(h, w_up, conv_w, conv_b, conv_state, layer):
    nb = N_CFF
    cws = lambda off: pl.BlockSpec((None, CONV_W, TC_FF), lambda n, m: (layer, 0, off + n))
    cbs = lambda off: pl.BlockSpec((None, 1, TC_FF), lambda n, m: (layer, 0, off + n))
    sts = lambda off: pl.BlockSpec((CONV_W - 1, N_SG, SB, TC_FF), lambda n, m: (0, 0, 0, off + n))
    ws = lambda off: pl.BlockSpec((None, D, TC_FF), lambda n, m: (layer, 0, off + n))
    cb3 = conv_b.reshape(DEPTH, 1, 2 * D_FF)
    return pl.pallas_call(
        _up_kernel,
        out_shape=(jax.ShapeDtypeStruct((T, D_FF), bf16), jax.ShapeDtypeStruct((TAIL_ROWS, D_FF), f32),
                   jax.ShapeDtypeStruct((TAIL_ROWS, D_FF), f32)),
        grid=(nb, N_MB),
        in_specs=[pl.BlockSpec((TM, D), lambda n, m: (m, 0)), ws(0), ws(nb), cws(0), cws(nb), cbs(0), cbs(nb),
                  sts(0), sts(nb)],
        out_specs=(pl.BlockSpec((TM, TC_FF), lambda n, m: (m, n)),
                   pl.BlockSpec((TAIL_ROWS, TC_FF), lambda n, m: (0, n)),
                   pl.BlockSpec((TAIL_ROWS, TC_FF), lambda n, m: (0, n))),
        scratch_shapes=[pltpu.VMEM((D, TC_FF), bf16), pltpu.VMEM((D, TC_FF), bf16),
                        pltpu.VMEM((8, TC_FF), f32), pltpu.VMEM((8, TC_FF), f32)],
        compiler_params=_cparams(("arbitrary", "arbitrary")),
        name=f"up_act{layer}",
    )(h, w_up, w_up, conv_w, conv_w, cb3, cb3, conv_state, conv_state)


def _ffn_kernel(act_ref, wd_ref, xm_ref, g2_ref, gain_ref, sh_ref, sc_ref, xo_ref, ho_ref, *, final):
    xo_ref[...] = _dot(act_ref[...], wd_ref[...])
    g2 = g2_ref[...]
    gain = gain_ref[...]
    for s in range(TR_FF // SB):
        rows = slice(s * SB, (s + 1) * SB)
        xn = xm_ref[rows, :] + g2 * xo_ref[rows, :]
        xo_ref[rows, :] = xn
        if final:
            ho_ref[rows, :] = _rms(xn) * gain
        else:
            ho_ref[rows, :] = (_rms(xn) * gain * (1.0 + sc_ref[...]) + sh_ref[...]).astype(bf16)


def _ffn(act, wd_bf, x_mid, mods, gain, layer, final):
    nxt = layer if final else layer + 1
    row = lambda i: (i, 0)
    return pl.pallas_call(
        functools.partial(_ffn_kernel, final=final),
        out_shape=(jax.ShapeDtypeStruct((T, D), f32), jax.ShapeDtypeStruct((T, D), f32 if final else bf16)),
        grid=(T // TR_FF,),
        in_specs=[
            pl.BlockSpec((TR_FF, D_FF), row),
            pl.BlockSpec((None, D_FF, D), lambda i: (layer, 0, 0), pipeline_mode=pl.Buffered(1)),
            pl.BlockSpec((TR_FF, D), row),
            _mod_spec(layer, _MOD_G2, TR_FF),
            pl.BlockSpec((1, D), lambda i: (0, 0)),
            _mod_spec(nxt, _MOD_SH1, TR_FF), _mod_spec(nxt, _MOD_SC1, TR_FF),
        ],
        out_specs=(pl.BlockSpec((TR_FF, D), row), pl.BlockSpec((TR_FF, D), row)),
        compiler_params=_cparams(("arbitrary",)),
        name=f"ffn{layer}",
    )(act, wd_bf, x_mid, mods, gain.reshape(1, D), mods, mods)


def _to_rows(a):
    f = a.shape[-1]
    return a.reshape(N_SG, SB, DEC_SEQ, f).transpose(0, 2, 1, 3).reshape(T_S, f)


def _from_rows(a, steps=DEC_SEQ):
    f = a.shape[-1]
    return a.reshape(N_SG, steps, SB, f).transpose(0, 2, 1, 3).reshape(DEC_BATCH, steps, f)


def kernel(x_prompt, x_sample, c_prompt, c_sample, state_hgrn, state_ffn_conv, ada_w, ada_b, norm1_g, norm2_g,
           w_in, gmlp_v_g, gmlp_ws, gmlp_bs, w_branch_a, hgrn_lower_bounds, hgrn_norm_g, w_branch_b, w_out,
           w_up, conv_w, conv_b, w_down, final_norm_g):
    x = jnp.concatenate([x_prompt.reshape(T_P, D), _to_rows(x_sample)], axis=0)
    c_all = jnp.concatenate([c_sample, c_prompt, jnp.zeros((4, D), f32)], axis=0)
    mods = _ada(c_all, ada_w, ada_b)

    wba_bf = w_branch_a.astype(bf16)
    wout_bf = w_out.astype(bf16)
    wd_bf = w_down.astype(bf16)
    conv_st = state_ffn_conv.transpose(0, 2, 1, 3).reshape(DEPTH, CONV_W - 1, N_SG, SB, 2 * D_FF)

    h = _norm_mod(x, norm1_g[0], mods, 0)
    hs_all = None
    hp, cp, cs, vp, vs = [], [], [], [], []
    y = None
    for l in range(DEPTH):
        zu, zv, zqo, zf, zi, zg = _inproj(h, w_in, hgrn_lower_bounds, gmlp_v_g, l)
        a_gated = _gmlp(zu, zv, zg, gmlp_ws[l], gmlp_bs[l], wba_bf[l], l)
        on, s_p = _hgrn_prompt(zqo, zf, zi, hgrn_norm_g, l)
        on, hs_all = _hgrn_sample(zqo, zf, zi, on, state_hgrn, hs_all, hgrn_norm_g, l)
        ymix = _branch_mix(on, w_branch_b, a_gated, zg, l)
        x_mid, h2 = _outproj(ymix, wout_bf, x, norm2_g, mods, l)
        act, tail_a, tail_b = _up_act(h2, w_up, conv_w, conv_b, conv_st[l], l)
        final = l == DEPTH - 1
        gain = final_norm_g if final else norm1_g[l + 1]
        x, hn = _ffn(act, wd_bf, x_mid, mods, gain, l, final)
        if final:
            y = hn
        else:
            h = hn
        tail = jnp.concatenate([tail_a, tail_b], axis=1)
        hp.append(s_p)
        cp.append(tail[:8 * BATCH].reshape(BATCH, 8, 2 * D_FF)[:, 8 - (CONV_W - 1):])
        cs.append(_from_rows(tail[8 * 16:], CONV_W - 1))
        vp.append(jnp.stack([zv[(b + 1) * SEQ - CHUNK:(b + 1) * SEQ] for b in range(BATCH)]))
        vs.append(_from_rows(zv[T_P:]))

    y_prompt = y[:T_P].reshape(BATCH, SEQ, D)
    y_sample = _from_rows(y[T_P:])
    return (y_prompt, y_sample, jnp.stack(hp), hs_all, jnp.stack(cp), jnp.stack(cs), jnp.stack(vp), jnp.stack(vs))
```

```python
import functools

import numpy as np
import jax
import jax.numpy as jnp
from jax import lax
from jax.experimental import pallas as pl
from jax.experimental.pallas import tpu as pltpu

f32 = jnp.float32
bf16 = jnp.bfloat16

D = 2048
BATCH = 4
SEQ = 2048
DEPTH = 2
DEC_BATCH = 128
DEC_SEQ = 8
G_A = 8
DG_A = 128
D_A = G_A * DG_A
CHUNK = 128
H_B = 16
DK = 128
DV = 128
D_FF = 5632
CONV_W = 3
N_IN = 2 * D_A + 6 * D
EPS = 1e-6

T_P = BATCH * SEQ
T_S = DEC_BATCH * DEC_SEQ
T = T_P + T_S
SB = 64
N_SG = DEC_BATCH // SB
TR = DEC_SEQ * SB
N_RB = T // TR
N_PRB = T_P // TR
RB_PER_SEQ = SEQ // TR
TM = 1024
N_MB = T // TM
N_PMB = T_P // TM
MB_PER_SEQ = SEQ // TM
TN = 1024
TC_FF = 512
N_CFF = D_FF // TC_FF
TR_FF = 256
HG_C = 128
HG_NH = 8
HG_BLOCK_LEVELS = (64, 32, 16)
HG_MASK_LEVELS = (8, 4, 2, 1)
HS_UNROLL = 16
LOG2E = 1.4426950408889634
TAIL_ROWS = 8 * 16 + (CONV_W - 1) * DEC_BATCH

_U0, _V0, _Q0, _F0, _I0, _OG0, _GA0 = 0, 1, 2, 4, 6, 8, 10
_MOD_SH1, _MOD_SC1, _MOD_G1, _MOD_SH2, _MOD_SC2, _MOD_G2 = range(6)
N_MODJ = BATCH + N_SG


def _cparams(sem):
    return pltpu.CompilerParams(dimension_semantics=sem)


def _hgrn_level_table(c):
    t = np.arange(c)
    lev = np.full((c, c), -1, np.int32)
    lev[t, t] = 0
    for h in HG_MASK_LEVELS:
        upper = (t % (2 * h)) >= h
        same = (t[:, None] // (2 * h)) == (t[None, :] // (2 * h))
        lev[same & upper[:, None] & ~upper[None, :]] = h
    return lev


_LEV_NP = _hgrn_level_table(HG_C)


def _rms(x, eps=EPS):
    return x * lax.rsqrt(jnp.mean(x * x, axis=-1, keepdims=True) + eps)


def _dot(a, b):
    return jnp.dot(a, b, preferred_element_type=f32)


def _dot_nt(a, b):
    return lax.dot_general(a, b, (((1,), (1,)), ((), ())), preferred_element_type=f32)


def _dot_tn(a, b):
    return lax.dot_general(a, b, (((0,), (0,)), ((), ())), preferred_element_type=f32)


def _modj(i, rows):
    return jnp.where(i < T_P // rows, i // (SEQ // rows), BATCH + (i - T_P // rows) // (TR // rows))


def _mod_spec(layer, kind, rows=TR):
    return pl.BlockSpec((None, None, None, SB, D), lambda i, *_: (layer, kind, _modj(i, rows), 0, 0))


def _ada_kernel(c_ref, w_ref, b_ref, o_ref):
    c = c_ref[...]
    a = (c * jax.nn.sigmoid(c)).astype(bf16)
    r = _dot(a, w_ref[...].astype(bf16)) + b_ref[...]
    for g in range(N_SG):
        o_ref[BATCH + g] = r[g * SB:(g + 1) * SB]
    for j in range(BATCH):
        o_ref[j] = jnp.broadcast_to(r[DEC_BATCH + j:DEC_BATCH + j + 1], (SB, TN))


def _ada(c_all, ada_w, ada_b):
    nrow = c_all.shape[0]
    ncol = 6 * D // TN
    per_kind = D // TN
    return pl.pallas_call(
        _ada_kernel,
        out_shape=jax.ShapeDtypeStruct((DEPTH, 6, N_MODJ, SB, D), f32),
        grid=(DEPTH, ncol),
        in_specs=[
            pl.BlockSpec((nrow, D), lambda l, n: (0, 0)),
            pl.BlockSpec((None, D, TN), lambda l, n: (l, 0, n)),
            pl.BlockSpec((None, 1, TN), lambda l, n: (l, 0, n)),
        ],
        out_specs=pl.BlockSpec((None, None, N_MODJ, SB, TN), lambda l, n: (l, n // per_kind, 0, 0, n % per_kind)),
        compiler_params=_cparams(("arbitrary", "arbitrary")),
        name="ada_mod",
    )(c_all, ada_w, ada_b.reshape(DEPTH, 1, 6 * D))


def _x_specs(rows):
    n_p = T_P // rows
    return [pl.BlockSpec((rows, D), lambda i: (jnp.minimum(i, n_p - 1), 0)),
            pl.BlockSpec((rows, D), lambda i: (jnp.maximum(i - n_p, 0), 0))]


def _norm_mod_kernel(xp_ref, xs_ref, g_ref, sh_ref, sc_ref, o_ref):
    g = g_ref[...]
    sh = sh_ref[...]
    sc1 = 1.0 + sc_ref[...]
    is_prompt = pl.program_id(0) < N_PRB
    for s in range(TR // SB):
        rows = slice(s * SB, (s + 1) * SB)
        x = jnp.where(is_prompt, xp_ref[rows, :], xs_ref[rows, :])
        o_ref[rows, :] = (_rms(x) * g * sc1 + sh).astype(bf16)


def _norm_mod(xp, xs, gain, mods, layer):
    return pl.pallas_call(
        _norm_mod_kernel,
        out_shape=jax.ShapeDtypeStruct((T, D), bf16),
        grid=(N_RB,),
        in_specs=[
            *_x_specs(TR),
            pl.BlockSpec((1, D), lambda i: (0, 0)),
            _mod_spec(layer, _MOD_SH1),
            _mod_spec(layer, _MOD_SC1),
        ],
        out_specs=pl.BlockSpec((TR, D), lambda i: (i, 0)),
        compiler_params=_cparams(("arbitrary",)),
        name="norm_mod",
    )(xp, xs, gain.reshape(1, D), mods, mods)


def _proj_kernel(h_ref, w_ref, *rest, epilogue):
    *extra, o_ref, wbf_ref = rest

    @pl.when(pl.program_id(1) == 0)
    def _():
        wbf_ref[...] = w_ref[...].astype(bf16)

    epilogue(_dot(h_ref[...], wbf_ref[...]), o_ref, *extra)


def _epi_gelu(acc, o_ref):
    o_ref[...] = jax.nn.gelu(acc).astype(o_ref.dtype)


def _epi_v(acc, o_ref, vg_ref):
    v = jax.nn.gelu(acc)
    for g in range(G_A):
        cols = slice(g * DG_A, (g + 1) * DG_A)
        o_ref[:, cols] = _rms(v[:, cols]) * vg_ref[:, cols]


def _epi_silu(acc, o_ref, scale_ref):
    o_ref[...] = (acc * jax.nn.sigmoid(acc) * scale_ref[pl.program_id(0)]).astype(o_ref.dtype)


def _epi_logf(acc, o_ref, lb_ref, *, layer):
    p = jax.nn.softmax(lb_ref[...], axis=0)
    lb = jnp.zeros((1, TN), f32)
    for j in range(1, layer + 1):
        lb = lb + p[j:j + 1]
    a = jnp.log(lb)
    b = jnp.log1p(-lb) + jnp.minimum(acc, 0.0) - jnp.log(1.0 + jnp.exp(-jnp.abs(acc)))
    o_ref[...] = jnp.maximum(a, b) + jnp.log(1.0 + jnp.exp(-jnp.abs(a - b)))


def _epi_id(acc, o_ref):
    o_ref[...] = acc.astype(o_ref.dtype)


def _epi_sigmoid(acc, o_ref):
    o_ref[...] = jax.nn.sigmoid(acc).astype(o_ref.dtype)


def _proj(name, h, w_in, layer, col_of, n_tiles, out_dtype, epilogue, extra=(), extra_specs=()):
    return pl.pallas_call(
        functools.partial(_proj_kernel, epilogue=epilogue),
        out_shape=jax.ShapeDtypeStruct((T, n_tiles * TN), out_dtype),
        grid=(n_tiles, N_MB),
        in_specs=[
            pl.BlockSpec((TM, D), lambda n, m: (m, 0)),
            pl.BlockSpec((None, D, TN), lambda n, m: (layer, 0, col_of(n))),
            *extra_specs,
        ],
        out_specs=pl.BlockSpec((TM, TN), lambda n, m: (m, n)),
        scratch_shapes=[pltpu.VMEM((D, TN), bf16)],
        compiler_params=_cparams(("arbitrary", "arbitrary")),
        name=f"{name}{layer}",
    )(h, w_in, *extra)


def _inproj(h, w_in, lower_bounds, v_gain, layer):
    smem = pl.BlockSpec(memory_space=pltpu.SMEM)
    zu = _proj("in_u", h, w_in, layer, lambda n: _U0 + n, 1, bf16, _epi_gelu)
    zv = _proj("in_v", h, w_in, layer, lambda n: _V0 + n, 1, f32, _epi_v,
               (v_gain.reshape(DEPTH, 1, D_A),), (pl.BlockSpec((None, 1, D_A), lambda n, m: (layer, 0, 0)),))
    scales = jnp.asarray([DK ** -0.5] * 2 + [1.0] * 2, f32)
    zqo = _proj("in_qo", h, w_in, layer, lambda n: jnp.where(n < 2, _Q0 + n, _OG0 + n - 2), 4, bf16, _epi_silu,
                (scales,), (smem,))
    zf = _proj("in_f", h, w_in, layer, lambda n: _F0 + n, 2, f32, functools.partial(_epi_logf, layer=layer),
               (lower_bounds,), (pl.BlockSpec((DEPTH, TN), lambda n, m: (0, n)),))
    zi = _proj("in_i", h, w_in, layer, lambda n: _I0 + n, 2, bf16, _epi_id)
    zg = _proj("in_g", h, w_in, layer, lambda n: _GA0 + n, 4, bf16, _epi_sigmoid)
    return zu, zv, zqo, zf, zi, zg


def _gmlp_kernel(u_ref, v_ref, ga_ref, ws_ref, bias_ref, w8_ref, b8_ref, wba_ref, o_ref, wm_ref, ya_ref):
    i = pl.program_id(0)

    @pl.when(i == 0)
    def _():
        r = lax.broadcasted_iota(jnp.int32, (CHUNK, CHUNK), 0)
        c = lax.broadcasted_iota(jnp.int32, (CHUNK, CHUNK), 1)
        for g in range(G_A):
            wm_ref[g] = jnp.where(r >= c, ws_ref[g], 0.0).astype(bf16)

    @pl.when(i < N_PRB)
    def _():
        for ch in range(TR // CHUNK):
            rows = slice(ch * CHUNK, (ch + 1) * CHUNK)
            for g in range(G_A):
                cols = slice(g * DG_A, (g + 1) * DG_A)
                s = _dot(wm_ref[g], v_ref[rows, cols].astype(bf16)) + bias_ref[:, cols]
                ya_ref[rows, cols] = (u_ref[rows, cols].astype(f32) * s).astype(bf16)

    @pl.when(i >= N_PRB)
    def _():
        for g in range(G_A):
            cols = slice(g * DG_A, (g + 1) * DG_A)
            vs = [v_ref[s * SB:(s + 1) * SB, cols] for s in range(DEC_SEQ)]
            for t in range(DEC_SEQ):
                acc = vs[0] * w8_ref[g * 64 + t * 8]
                for s in range(1, t + 1):
                    acc = acc + vs[s] * w8_ref[g * 64 + t * 8 + s]
                acc = acc + b8_ref[g * 8 + t]
                ya_ref[t * SB:(t + 1) * SB, cols] = (u_ref[t * SB:(t + 1) * SB, cols].astype(f32) * acc).astype(bf16)

    o_ref[...] = ga_ref[...].astype(f32) * _dot(ya_ref[...], wba_ref[...])


def _gmlp(zu, zv, zg, ws, bs, wba_bf, layer):
    bias_full = jnp.repeat(bs.T, DG_A, axis=1)
    w8 = ws[:, :DEC_SEQ, :DEC_SEQ].reshape(G_A * DEC_SEQ * DEC_SEQ)
    b8 = bs[:, :DEC_SEQ].reshape(G_A * DEC_SEQ)
    return pl.pallas_call(
        _gmlp_kernel,
        out_shape=jax.ShapeDtypeStruct((T, D), f32),
        grid=(N_RB,),
        in_specs=[
            pl.BlockSpec((TR, D_A), lambda i: (i, 0)),
            pl.BlockSpec((TR, D_A), lambda i: (i, 0)),
            pl.BlockSpec((TR, D), lambda i: (i, 0)),
            pl.BlockSpec((G_A, CHUNK, CHUNK), lambda i: (0, 0, 0)),
            pl.BlockSpec((CHUNK, D_A), lambda i: (0, 0)),
            pl.BlockSpec(memory_space=pltpu.SMEM),
            pl.BlockSpec(memory_space=pltpu.SMEM),
            pl.BlockSpec((D_A, D), lambda i: (0, 0)),
        ],
        out_specs=pl.BlockSpec((TR, D), lambda i: (i, 0)),
        scratch_shapes=[pltpu.VMEM((G_A, CHUNK, CHUNK), bf16), pltpu.VMEM((TR, D_A), bf16)],
        compiler_params=_cparams(("arbitrary",)),
        name=f"gmlp{layer}",
    )(zu, zv, zg, ws, bias_full, w8, b8, wba_bf)


def _bcast_row(x8, r):
    return jnp.broadcast_to(x8[r:r + 1, :], x8.shape)


def _cat_rows(tiles):
    return tiles[0] if len(tiles) == 1 else jnp.concatenate(tiles, axis=0)


def _hgrn_chunk(q, g, v, sts, lev, r8, nh):
    c = HG_C
    nv = c // 8
    w = nh * DK
    hs = [slice(i * DK, (i + 1) * DK) for i in range(nh)]
    gl = g * LOG2E
    x_t = []
    for j in range(nv):
        x = gl[8 * j:8 * j + 8, :]
        for sh in (1, 2, 4):
            x = x + jnp.where(r8 >= sh, pltpu.roll(x, sh, 0), 0.0)
        x_t.append(x)
    last = [_bcast_row(x, 7) for x in x_t]
    sh = 1
    while sh < nv:
        last = [last[j] + last[j - sh] if j >= sh else last[j] for j in range(nv)]
        sh *= 2
    a_t = [x_t[0]] + [x_t[j] + last[j - 1] for j in range(1, nv)]
    f = jnp.exp2(gl)
    kk = 1.0 - f
    q_t = [q[8 * j:8 * j + 8, :] for j in range(nv)]
    k_t = [kk[8 * j:8 * j + 8, :] for j in range(nv)]
    a = _cat_rows(a_t)
    odd = (lax.broadcasted_iota(jnp.int32, (c, w), 0) & 1) == 1

    qb = q.astype(bf16)
    kb0 = kk.astype(bf16)
    sc = [jnp.where(lev == 0, _dot_nt(qb[:, hh], kb0[:, hh]), 0.0) for hh in hs]
    for h in HG_MASK_LEVELS:
        if h == 8:
            e = _cat_rows([(a_t[j] - last[j - 1]) if j % 2 else (last[j] - a_t[j]) for j in range(nv)])
        elif h == 4:
            d = _cat_rows([t - _bcast_row(t, 3) for t in a_t])
            e = jnp.minimum(d, -d)
        elif h == 2:
            d = _cat_rows([t - jnp.where(r8 < 4, _bcast_row(t, 1), _bcast_row(t, 5)) for t in a_t])
            e = jnp.minimum(d, -d)
        if h >= 2:
            e = jnp.exp2(e)
        else:
            e = jnp.where(odd, f, 1.0)
        qe = (q * e).astype(bf16)
        ke = (kk * e).astype(bf16)
        sc = [jnp.where(lev == h, _dot_nt(qe[:, hh], ke[:, hh]), sc[i]) for i, hh in enumerate(hs)]
    sc_t = [[s_[8 * j:8 * j + 8, :] for j in range(nv)] for s_ in sc]
    zero8 = jnp.zeros((8, w), f32)
    for h in HG_BLOCK_LEVELS:
        ht = h // 8
        for blk in range(c // (2 * h)):
            lo = range(blk * 2 * ht, blk * 2 * ht + ht)
            up = range(blk * 2 * ht + ht, (blk + 1) * 2 * ht)
            ref = last[blk * 2 * ht + ht - 1]
            qt = _cat_rows([q_t[j] * jnp.exp2(a_t[j] - ref) for j in up]).astype(bf16)
            kz = _cat_rows([k_t[j] * jnp.exp2(ref - a_t[j]) if j in lo else zero8 for j in range(nv)]).astype(bf16)
            for i, hh in enumerate(hs):
                s_ = _dot_nt(qt[:, hh], kz[:, hh])
                for r, j in enumerate(up):
                    sc_t[i][j] = sc_t[i][j] + s_[8 * r:8 * r + 8, :]
    qa = (q * jnp.exp2(a)).astype(bf16)
    a_last = last[-1]
    kb = _cat_rows([k_t[j] * jnp.exp2(a_last - a_t[j]) for j in range(nv)]).astype(bf16)
    fl = jnp.exp2(a_last[0:1, :])
    o, st_new = [], []
    for i, hh in enumerate(hs):
        o.append(_dot_nt(qa[:, hh], sts[i].astype(bf16)) + _dot(_cat_rows(sc_t[i]).astype(bf16), v[:, hh]))
        st_new.append(fl[:, hh] * sts[i] + _dot_tn(v[:, hh], kb[:, hh]))
    return o, st_new


def _hgrn_prompt_kernel(q_ref, g_ref, v_ref, og_ref, lev_ref, gain_ref, on_ref, so_ref, st_ref):
    r8 = lax.broadcasted_iota(jnp.int32, (8, HG_NH * DK), 0)
    for hh in range(HG_NH):
        st_ref[hh] = jnp.zeros((DV, DK), f32)

    def chunk(c, carry):
        r = pl.ds(pl.multiple_of(c * HG_C, HG_C), HG_C)
        o, st_new = _hgrn_chunk(q_ref[r, :].astype(f32), g_ref[r, :], v_ref[r, :],
                                [st_ref[hh] for hh in range(HG_NH)], lev_ref[...], r8, HG_NH)
        for hh in range(HG_NH):
            cols = slice(hh * DV, (hh + 1) * DV)
            on_ref[r, cols] = (_rms(o[hh]) * gain_ref[:, cols] * og_ref[r, cols].astype(f32)).astype(bf16)
            st_ref[hh] = st_new[hh]
        return carry

    lax.fori_loop(0, SEQ // HG_C, chunk, 0)
    for hh in range(HG_NH):
        so_ref[hh] = st_ref[hh].T


def _hgrn_prompt(zqo, zf, zi, gain, layer):
    ng = H_B // HG_NH
    spec = lambda off: pl.BlockSpec((SEQ, HG_NH * DK), lambda b, h: (b, off + h))
    return pl.pallas_call(
        _hgrn_prompt_kernel,
        out_shape=(jax.ShapeDtypeStruct((T, D), bf16), jax.ShapeDtypeStruct((BATCH, H_B, DK, DV), f32)),
        grid=(BATCH, ng),
        in_specs=[
            spec(0), spec(0), spec(0), spec(ng),
            pl.BlockSpec((HG_C, HG_C), lambda b, h: (0, 0)),
            pl.BlockSpec((None, 1, HG_NH * DV), lambda b, h: (layer, 0, h)),
        ],
        out_specs=(pl.BlockSpec((SEQ, HG_NH * DV), lambda b, h: (b, h)),
                   pl.BlockSpec((None, HG_NH, DK, DV), lambda b, h: (b, h, 0, 0))),
        scratch_shapes=[pltpu.VMEM((HG_NH, DV, DK), f32)],
        compiler_params=_cparams(("arbitrary", "arbitrary")),
        name=f"hgrn_prompt{layer}",
    )(zqo, zf, zi, zqo, jnp.asarray(_LEV_NP), gain.reshape(DEPTH, 1, D))


def _hgrn_sample_kernel(*refs):
    q_ref, g_ref, v_ref, og_ref, s_ref, gain_ref = refs[:6]
    on_ref, so_ref, qa_scr, kb_scr, v_scr, fs_scr, o_scr = refs[-7:]
    slab = lambda t: slice(t * SB, (t + 1) * SB)
    a, q, kk, v = [], [], [], []
    for t in range(DEC_SEQ):
        gl = g_ref[slab(t), :] * LOG2E
        a.append(gl if t == 0 else a[t - 1] + gl)
        kk.append(1.0 - jnp.exp2(gl))
        q.append(q_ref[slab(t), :].astype(f32))
        v.append(v_ref[slab(t), :].astype(f32))
    a_last = a[DEC_SEQ - 1]
    for t in range(DEC_SEQ):
        qa_scr[slab(t), :] = q[t] * jnp.exp2(a[t])
        kb_scr[slab(t), :] = kk[t] * jnp.exp2(a_last - a[t])
        v_scr[slab(t), :] = v[t]
        o = jnp.sum(q[t] * kk[t], axis=-1, keepdims=True) * v[t]
        for s in range(t):
            o = o + jnp.sum(q[t] * kk[s] * jnp.exp2(a[t] - a[s]), axis=-1, keepdims=True) * v[s]
        o_scr[slab(t), :] = o
    fl = jnp.exp2(a_last)
    f1 = fl.astype(bf16).astype(f32)
    r1 = fl - f1
    f2 = r1.astype(bf16).astype(f32)
    fs_scr[slab(0), :] = f1
    fs_scr[slab(1), :] = f2
    fs_scr[slab(2), :] = r1 - f2
    fs_scr[3 * SB:, :] = jnp.zeros((TR - 3 * SB, DK), f32)

    ones3 = (lax.broadcasted_iota(jnp.int32, (DEC_SEQ, DV), 0) < 3).astype(bf16)
    zero8 = jnp.zeros((DEC_SEQ, DV), bf16)

    def per_batch(b, carry):
        rows = pl.ds(b, DEC_SEQ, stride=SB)
        s0 = s_ref[b]
        o_scr[rows, :] = o_scr[rows, :] + _dot(qa_scr[rows, :].astype(bf16), s0.astype(bf16))
        lhs = jnp.concatenate([kb_scr[rows, :], fs_scr[rows, :]], axis=0).astype(bf16)
        v8 = v_scr[rows, :].astype(bf16)
        rhs = jnp.concatenate([jnp.concatenate([v8, zero8], axis=1), jnp.concatenate([zero8, ones3], axis=1)], axis=0)
        r = _dot_tn(lhs, rhs)
        so_ref[b] = r[:, DV:] * s0 + r[:, :DV]
        return carry

    lax.fori_loop(0, SB, per_batch, 0, unroll=HS_UNROLL)
    gain = gain_ref[...]
    for t in range(DEC_SEQ):
        on_ref[slab(t), :] = (_rms(o_scr[slab(t), :]) * gain * og_ref[slab(t), :].astype(f32)).astype(bf16)


def _hgrn_sample(zqo, zf, zi, on, state_hgrn, hs_prev, gain, layer):
    rb0 = T_P // TR
    spec = lambda off: pl.BlockSpec((TR, DK), lambda g, h: (rb0 + g, off + h))
    sspec = pl.BlockSpec((None, SB, None, DK, DV), lambda g, h: (layer, g, h, 0, 0))
    ins = [zqo, zf, zi, zqo, state_hgrn, gain.reshape(DEPTH, 1, D), on]
    in_specs = [spec(0), spec(0), spec(0), spec(H_B), sspec,
                pl.BlockSpec((None, 1, DV), lambda g, h: (layer, 0, h)),
                pl.BlockSpec(memory_space=pl.ANY)]
    aliases = {6: 0}
    if hs_prev is not None:
        ins.append(hs_prev)
        in_specs.append(pl.BlockSpec(memory_space=pl.ANY))
        aliases[7] = 1
    return pl.pallas_call(
        _hgrn_sample_kernel,
        out_shape=(jax.ShapeDtypeStruct((T, D), bf16), jax.ShapeDtypeStruct(state_hgrn.shape, f32)),
        grid=(N_SG, H_B),
        in_specs=in_specs,
        out_specs=(pl.BlockSpec((TR, DV), lambda g, h: (rb0 + g, h)), sspec),
        scratch_shapes=[pltpu.VMEM((TR, DV), f32)] * 5,
        input_output_aliases=aliases,
        compiler_params=_cparams(("arbitrary", "arbitrary")),
        name=f"hgrn_sample{layer}",
    )(*ins)


def _mix_kernel(on_ref, w_ref, a_ref, gb_ref, o_ref, wbf_ref):
    @pl.when(pl.program_id(1) == 0)
    def _():
        wbf_ref[...] = w_ref[...].astype(bf16)

    o_ref[...] = (a_ref[...] + gb_ref[...].astype(f32) * _dot(on_ref[...], wbf_ref[...])).astype(bf16)


def _branch_mix(on, w_bb, a_gated, zg, layer):
    return pl.pallas_call(
        _mix_kernel,
        out_shape=jax.ShapeDtypeStruct((T, D), bf16),
        grid=(D // TN, N_MB),
        in_specs=[
            pl.BlockSpec((TM, D), lambda n, m: (m, 0)),
            pl.BlockSpec((None, D, TN), lambda n, m: (layer, 0, n)),
            pl.BlockSpec((TM, TN), lambda n, m: (m, n)),
            pl.BlockSpec((TM, TN), lambda n, m: (m, D // TN + n)),
        ],
        out_specs=pl.BlockSpec((TM, TN), lambda n, m: (m, n)),
        scratch_shapes=[pltpu.VMEM((D, TN), bf16)],
        compiler_params=_cparams(("arbitrary", "arbitrary")),
        name=f"branch_mix{layer}",
    )(on, w_bb, a_gated, zg)


def _outproj_kernel(y_ref, w_ref, *rest, split_x):
    *x_refs, n2_ref, g1_ref, sh_ref, sc_ref, xo_ref, h_ref = rest
    xo_ref[...] = _dot(y_ref[...], w_ref[...])
    n2 = n2_ref[...]
    g1 = g1_ref[...]
    sh = sh_ref[...]
    sc1 = 1.0 + sc_ref[...]
    is_prompt = pl.program_id(0) < N_PRB
    for s in range(TR // SB):
        rows = slice(s * SB, (s + 1) * SB)
        x = jnp.where(is_prompt, x_refs[0][rows, :], x_refs[1][rows, :]) if split_x else x_refs[0][rows, :]
        xm = x + g1 * xo_ref[rows, :]
        xo_ref[rows, :] = xm
        h_ref[rows, :] = (_rms(xm) * n2 * sc1 + sh).astype(bf16)


def _outproj(y, w_out_bf, xs, n2, mods, layer):
    split_x = len(xs) == 2
    return pl.pallas_call(
        functools.partial(_outproj_kernel, split_x=split_x),
        out_shape=(jax.ShapeDtypeStruct((T, D), f32), jax.ShapeDtypeStruct((T, D), bf16)),
        grid=(N_RB,),
        in_specs=[
            pl.BlockSpec((TR, D), lambda i: (i, 0)),
            pl.BlockSpec((None, D, D), lambda i: (layer, 0, 0), pipeline_mode=pl.Buffered(1)),
            *(_x_specs(TR) if split_x else [pl.BlockSpec((TR, D), lambda i: (i, 0))]),
            pl.BlockSpec((None, 1, D), lambda i: (layer, 0, 0)),
            _mod_spec(layer, _MOD_G1), _mod_spec(layer, _MOD_SH2), _mod_spec(layer, _MOD_SC2),
        ],
        out_specs=(pl.BlockSpec((TR, D), lambda i: (i, 0)), pl.BlockSpec((TR, D), lambda i: (i, 0))),
        compiler_params=_cparams(("arbitrary",)),
        name=f"outproj{layer}",
    )(y, w_out_bf, *xs, n2.reshape(DEPTH, 1, D), mods, mods, mods)


def _conv3(cur, m1, m2, cw_ref, cb_ref):
    return cb_ref[...] + cw_ref[0:1, :] * m2 + cw_ref[1:2, :] * m1 + cw_ref[2:3, :] * cur


def _up_kernel(h_ref, wa_ref, wb_ref, cwa_ref, cwb_ref, cba_ref, cbb_ref, sa_ref, sb_ref,
               act_ref, ta_ref, tb_ref, wabf_ref, wbbf_ref, haloa_ref, halob_ref):
    m = pl.program_id(1)

    @pl.when(m == 0)
    def _():
        wabf_ref[...] = wa_ref[...].astype(bf16)
        wbbf_ref[...] = wb_ref[...].astype(bf16)
        ta_ref[...] = jnp.zeros_like(ta_ref)
        tb_ref[...] = jnp.zeros_like(tb_ref)
        haloa_ref[...] = jnp.zeros_like(haloa_ref)
        halob_ref[...] = jnp.zeros_like(halob_ref)

    h = h_ref[...]
    acc_a = _dot(h, wabf_ref[...])
    acc_b = _dot(h, wbbf_ref[...])
    seq_start = (m % MB_PER_SEQ) == 0
    r8 = lax.broadcasted_iota(jnp.int32, (8, TC_FF), 0)

    def conv(cur, halo_ref, cw_ref, cb_ref):
        body = _conv3(cur, pltpu.roll(cur, 1, 0), pltpu.roll(cur, 2, 0), cw_ref, cb_ref)
        halo = jnp.where(seq_start, 0.0, halo_ref[...])
        cur8 = cur[0:8, :]
        m1 = jnp.where(r8 < 1, pltpu.roll(halo, 1, 0), pltpu.roll(cur8, 1, 0))
        m2 = jnp.where(r8 < 2, pltpu.roll(halo, 2, 0), pltpu.roll(cur8, 2, 0))
        return body, _conv3(cur8, m1, m2, cw_ref, cb_ref)

    ca, ca8 = conv(acc_a, haloa_ref, cwa_ref, cba_ref)
    cb, cb8 = conv(acc_b, halob_ref, cwb_ref, cbb_ref)
    act_ref[...] = (jax.nn.gelu(ca) * cb).astype(bf16)
    act_ref[0:8, :] = (jax.nn.gelu(ca8) * cb8).astype(bf16)
    ta8 = acc_a[TM - 8:TM, :]
    tb8 = acc_b[TM - 8:TM, :]
    haloa_ref[...] = ta8
    halob_ref[...] = tb8

    @pl.when((m < N_PMB) & (m % MB_PER_SEQ == MB_PER_SEQ - 1))
    def _():
        r = pl.ds(pl.multiple_of((m // MB_PER_SEQ) * 8, 8), 8)
        ta_ref[r, :] = ta8
        tb_ref[r, :] = tb8

    @pl.when(m == N_PMB)
    def _():
        def conv_s(acc, st_ref, cw_ref, cb_ref, g, t):
            slab = lambda j: acc[(g * DEC_SEQ + j) * SB:(g * DEC_SEQ + j + 1) * SB, :]
            m1 = slab(t - 1) if t >= 1 else st_ref[1, g]
            m2 = slab(t - 2) if t >= 2 else st_ref[t, g]
            return _conv3(slab(t), m1, m2, cw_ref, cb_ref)

        for g in range(N_SG):
            for t in range(DEC_SEQ):
                ca_ = conv_s(acc_a, sa_ref, cwa_ref, cba_ref, g, t)
                cb_ = conv_s(acc_b, sb_ref, cwb_ref, cbb_ref, g, t)
                r0 = (g * DEC_SEQ + t) * SB
                act_ref[r0:r0 + SB, :] = (jax.nn.gelu(ca_) * cb_).astype(bf16)
            for j in range(CONV_W - 1):
                r0 = (g * DEC_SEQ + DEC_SEQ - (CONV_W - 1) + j) * SB
                d0 = 8 * 16 + (g * (CONV_W - 1) + j) * SB
                ta_ref[d0:d0 + SB, :] = acc_a[r0:r0 + SB, :]
                tb_ref[d0:d0 + SB, :] = acc_b[r0:r0 + SB, :]


def _up_act(h, w_up, conv_w, conv_b, conv_state, layer):
    nb = N_CFF
    cws = lambda off: pl.BlockSpec((None, CONV_W, TC_FF), lambda n, m: (layer, 0, off + n))
    cbs = lambda off: pl.BlockSpec((None, 1, TC_FF), lambda n, m: (layer, 0, off + n))
    sts = lambda off: pl.BlockSpec((CONV_W - 1, N_SG, SB, TC_FF), lambda n, m: (0, 0, 0, off + n))
    ws = lambda off: pl.BlockSpec((None, D, TC_FF), lambda n, m: (layer, 0, off + n))
    cb3 = conv_b.reshape(DEPTH, 1, 2 * D_FF)
    return pl.pallas_call(
        _up_kernel,
        out_shape=(jax.ShapeDtypeStruct((T, D_FF), bf16), jax.ShapeDtypeStruct((TAIL_ROWS, D_FF), f32),
                   jax.ShapeDtypeStruct((TAIL_ROWS, D_FF), f32)),
        grid=(nb, N_MB),
        in_specs=[pl.BlockSpec((TM, D), lambda n, m: (m, 0)), ws(0), ws(nb), cws(0), cws(nb), cbs(0), cbs(nb),
                  sts(0), sts(nb)],
        out_specs=(pl.BlockSpec((TM, TC_FF), lambda n, m: (m, n)),
                   pl.BlockSpec((TAIL_ROWS, TC_FF), lambda n, m: (0, n)),
                   pl.BlockSpec((TAIL_ROWS, TC_FF), lambda n, m: (0, n))),
        scratch_shapes=[pltpu.VMEM((D, TC_FF), bf16), pltpu.VMEM((D, TC_FF), bf16),
                        pltpu.VMEM((8, TC_FF), f32), pltpu.VMEM((8, TC_FF), f32)],
        compiler_params=_cparams(("arbitrary", "arbitrary")),
        name=f"up_act{layer}",
    )(h, w_up, w_up, conv_w, conv_w, cb3, cb3, conv_state, conv_state)


def _ffn_kernel(act_ref, wd_ref, xm_ref, g2_ref, gain_ref, sh_ref, sc_ref, xo_ref, ho_ref):
    xo_ref[...] = _dot(act_ref[...], wd_ref[...])
    g2 = g2_ref[...]
    gain = gain_ref[...]
    for s in range(TR_FF // SB):
        rows = slice(s * SB, (s + 1) * SB)
        xn = xm_ref[rows, :] + g2 * xo_ref[rows, :]
        xo_ref[rows, :] = xn
        ho_ref[rows, :] = (_rms(xn) * gain * (1.0 + sc_ref[...]) + sh_ref[...]).astype(bf16)


def _ffn_final_kernel(act_ref, wd_ref, xm_ref, g2_ref, gain_ref, yp_ref, ys_ref, y_scr):
    y_scr[...] = _dot(act_ref[...], wd_ref[...])
    g2 = g2_ref[...]
    gain = gain_ref[...]
    for s in range(TR_FF // SB):
        rows = slice(s * SB, (s + 1) * SB)
        y_scr[rows, :] = _rms(xm_ref[rows, :] + g2 * y_scr[rows, :]) * gain
    is_prompt = pl.program_id(0) < T_P // TR_FF

    @pl.when(is_prompt)
    def _():
        yp_ref[...] = y_scr[...]

    @pl.when(jnp.logical_not(is_prompt))
    def _():
        ys_ref[...] = y_scr[...]


def _ffn(act, wd_bf, x_mid, mods, gain, layer, final):
    row = lambda i: (i, 0)
    n_p = T_P // TR_FF
    in_specs = [
        pl.BlockSpec((TR_FF, D_FF), row),
        pl.BlockSpec((None, D_FF, D), lambda i: (layer, 0, 0), pipeline_mode=pl.Buffered(1)),
        pl.BlockSpec((TR_FF, D), row),
        _mod_spec(layer, _MOD_G2, TR_FF),
        pl.BlockSpec((1, D), lambda i: (0, 0)),
    ]
    if final:
        return pl.pallas_call(
            _ffn_final_kernel,
            out_shape=(jax.ShapeDtypeStruct((T_P, D), f32), jax.ShapeDtypeStruct((T_S, D), f32)),
            grid=(T // TR_FF,),
            in_specs=in_specs,
            out_specs=(pl.BlockSpec((TR_FF, D), lambda i: (jnp.minimum(i, n_p - 1), 0)),
                       pl.BlockSpec((TR_FF, D), lambda i: (jnp.maximum(i - n_p, 0), 0))),
            scratch_shapes=[pltpu.VMEM((TR_FF, D), f32)],
            compiler_params=_cparams(("arbitrary",)),
            name=f"ffn{layer}",
        )(act, wd_bf, x_mid, mods, gain.reshape(1, D))
    return pl.pallas_call(
        _ffn_kernel,
        out_shape=(jax.ShapeDtypeStruct((T, D), f32), jax.ShapeDtypeStruct((T, D), bf16)),
        grid=(T // TR_FF,),
        in_specs=in_specs + [_mod_spec(layer + 1, _MOD_SH1, TR_FF), _mod_spec(layer + 1, _MOD_SC1, TR_FF)],
        out_specs=(pl.BlockSpec((TR_FF, D), row), pl.BlockSpec((TR_FF, D), row)),
        compiler_params=_cparams(("arbitrary",)),
        name=f"ffn{layer}",
    )(act, wd_bf, x_mid, mods, gain.reshape(1, D), mods, mods)


def _to_rows(a):
    f = a.shape[-1]
    return a.reshape(N_SG, SB, DEC_SEQ, f).transpose(0, 2, 1, 3).reshape(T_S, f)


def _from_rows(a, steps=DEC_SEQ):
    f = a.shape[-1]
    return a.reshape(N_SG, steps, SB, f).transpose(0, 2, 1, 3).reshape(DEC_BATCH, steps, f)


def kernel(x_prompt, x_sample, c_prompt, c_sample, state_hgrn, state_ffn_conv, ada_w, ada_b, norm1_g, norm2_g,
           w_in, gmlp_v_g, gmlp_ws, gmlp_bs, w_branch_a, hgrn_lower_bounds, hgrn_norm_g, w_branch_b, w_out,
           w_up, conv_w, conv_b, w_down, final_norm_g):
    xs = (x_prompt.reshape(T_P, D), _to_rows(x_sample))
    c_all = jnp.concatenate([c_sample, c_prompt, jnp.zeros((4, D), f32)], axis=0)
    mods = _ada(c_all, ada_w, ada_b)

    wba_bf = w_branch_a.astype(bf16)
    wout_bf = w_out.astype(bf16)
    wd_bf = w_down.astype(bf16)
    conv_st = state_ffn_conv.transpose(0, 2, 1, 3).reshape(DEPTH, CONV_W - 1, N_SG, SB, 2 * D_FF)

    h = _norm_mod(*xs, norm1_g[0], mods, 0)
    hs_all = None
    hp, cp, cs, vp, vs = [], [], [], [], []
    y_p = y_s = None
    for l in range(DEPTH):
        zu, zv, zqo, zf, zi, zg = _inproj(h, w_in, hgrn_lower_bounds, gmlp_v_g, l)
        a_gated = _gmlp(zu, zv, zg, gmlp_ws[l], gmlp_bs[l], wba_bf[l], l)
        on, s_p = _hgrn_prompt(zqo, zf, zi, hgrn_norm_g, l)
        on, hs_all = _hgrn_sample(zqo, zf, zi, on, state_hgrn, hs_all, hgrn_norm_g, l)
        ymix = _branch_mix(on, w_branch_b, a_gated, zg, l)
        x_mid, h2 = _outproj(ymix, wout_bf, xs, norm2_g, mods, l)
        act, tail_a, tail_b = _up_act(h2, w_up, conv_w, conv_b, conv_st[l], l)
        if l == DEPTH - 1:
            y_p, y_s = _ffn(act, wd_bf, x_mid, mods, final_norm_g, l, True)
        else:
            x, h = _ffn(act, wd_bf, x_mid, mods, norm1_g[l + 1], l, False)
            xs = (x,)
        tail = jnp.concatenate([tail_a, tail_b], axis=1)
        hp.append(s_p)
        cp.append(tail[:8 * BATCH].reshape(BATCH, 8, 2 * D_FF)[:, 8 - (CONV_W - 1):])
        cs.append(_from_rows(tail[8 * 16:], CONV_W - 1))
        vp.append(jnp.stack([zv[(b + 1) * SEQ - CHUNK:(b + 1) * SEQ] for b in range(BATCH)]))
        vs.append(_from_rows(zv[T_P:]))

    y_prompt = y_p.reshape(BATCH, SEQ, D)
    y_sample = _from_rows(y_s)
    return (y_prompt, y_sample, jnp.stack(hp), hs_all, jnp.stack(cp), jnp.stack(cs), jnp.stack(vp), jnp.stack(vs))
```

```python
import functools

import numpy as np
import jax
import jax.numpy as jnp
from jax import lax
from jax.experimental import pallas as pl
from jax.experimental.pallas import tpu as pltpu

f32 = jnp.float32
bf16 = jnp.bfloat16

D = 2048
BATCH = 4
SEQ = 2048
DEPTH = 2
DEC_BATCH = 128
DEC_SEQ = 8
G_A = 8
DG_A = 128
D_A = G_A * DG_A
CHUNK = 128
H_B = 16
DK = 128
DV = 128
D_FF = 5632
CONV_W = 3
N_IN = 2 * D_A + 6 * D
EPS = 1e-6

T_P = BATCH * SEQ
T_S = DEC_BATCH * DEC_SEQ
T = T_P + T_S
SB = 64
N_SG = DEC_BATCH // SB
TR = DEC_SEQ * SB
N_RB = T // TR
N_PRB = T_P // TR
RB_PER_SEQ = SEQ // TR
TM_IN = 1536
TM = 1024
N_MB = T // TM
N_PMB = T_P // TM
MB_PER_SEQ = SEQ // TM
TN = 1024
TC_FF = 512
N_CFF = D_FF // TC_FF
TR_FF = 256
HG_C = 128
HG_NH = 8
HG_BLOCK_LEVELS = (64, 32, 16)
HG_MASK_LEVELS = (8, 4, 2, 1)
HS_UNROLL = 16
LOG2E = 1.4426950408889634
TAIL_ROWS = 8 * 16 + (CONV_W - 1) * DEC_BATCH

_U0, _V0, _Q0, _F0, _I0, _OG0, _GA0 = 0, 1, 2, 4, 6, 8, 10
_MOD_SH1, _MOD_SC1, _MOD_G1, _MOD_SH2, _MOD_SC2, _MOD_G2 = range(6)
N_MODJ = BATCH + N_SG


def _cparams(sem):
    return pltpu.CompilerParams(dimension_semantics=sem)


def _hgrn_level_table(c):
    t = np.arange(c)
    lev = np.full((c, c), -1, np.int32)
    lev[t, t] = 0
    for h in HG_MASK_LEVELS:
        upper = (t % (2 * h)) >= h
        same = (t[:, None] // (2 * h)) == (t[None, :] // (2 * h))
        lev[same & upper[:, None] & ~upper[None, :]] = h
    return lev


_LEV_NP = _hgrn_level_table(HG_C)


def _rms(x, eps=EPS):
    return x * lax.rsqrt(jnp.mean(x * x, axis=-1, keepdims=True) + eps)


def _dot(a, b):
    return jnp.dot(a, b, preferred_element_type=f32)


def _dot_nt(a, b):
    return lax.dot_general(a, b, (((1,), (1,)), ((), ())), preferred_element_type=f32)


def _dot_tn(a, b):
    return lax.dot_general(a, b, (((0,), (0,)), ((), ())), preferred_element_type=f32)


def _modj(i, rows):
    return jnp.where(i < T_P // rows, i // (SEQ // rows), BATCH + (i - T_P // rows) // (TR // rows))


def _mod_spec(layer, kind, rows=TR):
    return pl.BlockSpec((None, None, None, SB, D), lambda i, *_: (layer, kind, _modj(i, rows), 0, 0))


def _ada_kernel(c_ref, w_ref, b_ref, o_ref):
    c = c_ref[...]
    a = (c * jax.nn.sigmoid(c)).astype(bf16)
    r = _dot(a, w_ref[...].astype(bf16)) + b_ref[...]
    for g in range(N_SG):
        o_ref[BATCH + g] = r[g * SB:(g + 1) * SB]
    for j in range(BATCH):
        o_ref[j] = jnp.broadcast_to(r[DEC_BATCH + j:DEC_BATCH + j + 1], (SB, TN))


def _ada(c_all, ada_w, ada_b):
    nrow = c_all.shape[0]
    ncol = 6 * D // TN
    per_kind = D // TN
    return pl.pallas_call(
        _ada_kernel,
        out_shape=jax.ShapeDtypeStruct((DEPTH, 6, N_MODJ, SB, D), f32),
        grid=(DEPTH, ncol),
        in_specs=[
            pl.BlockSpec((nrow, D), lambda l, n: (0, 0)),
            pl.BlockSpec((None, D, TN), lambda l, n: (l, 0, n)),
            pl.BlockSpec((None, 1, TN), lambda l, n: (l, 0, n)),
        ],
        out_specs=pl.BlockSpec((None, None, N_MODJ, SB, TN), lambda l, n: (l, n // per_kind, 0, 0, n % per_kind)),
        compiler_params=_cparams(("arbitrary", "arbitrary")),
        name="ada_mod",
    )(c_all, ada_w, ada_b.reshape(DEPTH, 1, 6 * D))


def _x_specs(rows):
    n_p = T_P // rows
    return [pl.BlockSpec((rows, D), lambda i: (jnp.minimum(i, n_p - 1), 0)),
            pl.BlockSpec((rows, D), lambda i: (jnp.maximum(i - n_p, 0), 0))]


def _norm_mod_kernel(xp_ref, xs_ref, g_ref, sh_ref, sc_ref, o_ref):
    g = g_ref[...]
    sh = sh_ref[...]
    sc1 = 1.0 + sc_ref[...]
    is_prompt = pl.program_id(0) < N_PRB
    for s in range(TR // SB):
        rows = slice(s * SB, (s + 1) * SB)
        x = jnp.where(is_prompt, xp_ref[rows, :], xs_ref[rows, :])
        o_ref[rows, :] = (_rms(x) * g * sc1 + sh).astype(bf16)


def _norm_mod(xp, xs, gain, mods, layer):
    return pl.pallas_call(
        _norm_mod_kernel,
        out_shape=jax.ShapeDtypeStruct((T, D), bf16),
        grid=(N_RB,),
        in_specs=[
            *_x_specs(TR),
            pl.BlockSpec((1, D), lambda i: (0, 0)),
            _mod_spec(layer, _MOD_SH1),
            _mod_spec(layer, _MOD_SC1),
        ],
        out_specs=pl.BlockSpec((TR, D), lambda i: (i, 0)),
        compiler_params=_cparams(("arbitrary",)),
        name="norm_mod",
    )(xp, xs, gain.reshape(1, D), mods, mods)


def _proj_kernel(h_ref, w_ref, *rest, epilogue):
    *extra, o_ref, wbf_ref = rest

    @pl.when(pl.program_id(1) == 0)
    def _():
        wbf_ref[...] = w_ref[...].astype(bf16)

    epilogue(_dot(h_ref[...], wbf_ref[...]), o_ref, *extra)


def _epi_gelu(acc, o_ref):
    o_ref[...] = jax.nn.gelu(acc).astype(o_ref.dtype)


def _epi_v(acc, o_ref, vg_ref):
    v = jax.nn.gelu(acc)
    for g in range(G_A):
        cols = slice(g * DG_A, (g + 1) * DG_A)
        o_ref[:, cols] = _rms(v[:, cols]) * vg_ref[:, cols]


def _epi_silu(acc, o_ref, scale_ref):
    o_ref[...] = (acc * jax.nn.sigmoid(acc) * scale_ref[pl.program_id(0)]).astype(o_ref.dtype)


def _epi_logf(acc, o_ref, lb_ref, *, layer):
    p = jax.nn.softmax(lb_ref[...], axis=0)
    lb = jnp.zeros((1, TN), f32)
    for j in range(1, layer + 1):
        lb = lb + p[j:j + 1]
    a = jnp.log(lb)
    b = jnp.log1p(-lb) + jnp.minimum(acc, 0.0) - jnp.log(1.0 + jnp.exp(-jnp.abs(acc)))
    o_ref[...] = jnp.maximum(a, b) + jnp.log(1.0 + jnp.exp(-jnp.abs(a - b)))


def _epi_id(acc, o_ref):
    o_ref[...] = acc.astype(o_ref.dtype)


def _epi_sigmoid(acc, o_ref):
    o_ref[...] = jax.nn.sigmoid(acc).astype(o_ref.dtype)


def _proj(name, h, w_in, layer, col_of, n_tiles, out_dtype, epilogue, extra=(), extra_specs=()):
    return pl.pallas_call(
        functools.partial(_proj_kernel, epilogue=epilogue),
        out_shape=jax.ShapeDtypeStruct((T, n_tiles * TN), out_dtype),
        grid=(n_tiles, T // TM_IN),
        in_specs=[
            pl.BlockSpec((TM_IN, D), lambda n, m: (m, 0)),
            pl.BlockSpec((None, D, TN), lambda n, m: (layer, 0, col_of(n))),
            *extra_specs,
        ],
        out_specs=pl.BlockSpec((TM_IN, TN), lambda n, m: (m, n)),
        scratch_shapes=[pltpu.VMEM((D, TN), bf16)],
        compiler_params=_cparams(("arbitrary", "arbitrary")),
        name=f"{name}{layer}",
    )(h, w_in, *extra)


def _inproj(h, w_in, lower_bounds, v_gain, layer):
    smem = pl.BlockSpec(memory_space=pltpu.SMEM)
    zu = _proj("in_u", h, w_in, layer, lambda n: _U0 + n, 1, bf16, _epi_gelu)
    zv = _proj("in_v", h, w_in, layer, lambda n: _V0 + n, 1, f32, _epi_v,
               (v_gain.reshape(DEPTH, 1, D_A),), (pl.BlockSpec((None, 1, D_A), lambda n, m: (layer, 0, 0)),))
    scales = jnp.asarray([DK ** -0.5] * 2 + [1.0] * 2, f32)
    zqo = _proj("in_qo", h, w_in, layer, lambda n: jnp.where(n < 2, _Q0 + n, _OG0 + n - 2), 4, bf16, _epi_silu,
                (scales,), (smem,))
    zf = _proj("in_f", h, w_in, layer, lambda n: _F0 + n, 2, f32, functools.partial(_epi_logf, layer=layer),
               (lower_bounds,), (pl.BlockSpec((DEPTH, TN), lambda n, m: (0, n)),))
    zi = _proj("in_i", h, w_in, layer, lambda n: _I0 + n, 2, bf16, _epi_id)
    zg = _proj("in_g", h, w_in, layer, lambda n: _GA0 + n, 4, bf16, _epi_sigmoid)
    return zu, zv, zqo, zf, zi, zg


def _gmlp_kernel(u_ref, v_ref, ga_ref, ws_ref, bias_ref, w8_ref, b8_ref, wba_ref, o_ref, wm_ref, ya_ref, wbf_ref):
    i = pl.program_id(0)

    @pl.when(i == 0)
    def _():
        wbf_ref[...] = wba_ref[...].astype(bf16)
        r = lax.broadcasted_iota(jnp.int32, (CHUNK, CHUNK), 0)
        c = lax.broadcasted_iota(jnp.int32, (CHUNK, CHUNK), 1)
        for g in range(G_A):
            wm_ref[g] = jnp.where(r >= c, ws_ref[g], 0.0).astype(bf16)

    @pl.when(i < N_PRB)
    def _():
        for ch in range(TR // CHUNK):
            rows = slice(ch * CHUNK, (ch + 1) * CHUNK)
            for g in range(G_A):
                cols = slice(g * DG_A, (g + 1) * DG_A)
                s = _dot(wm_ref[g], v_ref[rows, cols].astype(bf16)) + bias_ref[:, cols]
                ya_ref[rows, cols] = (u_ref[rows, cols].astype(f32) * s).astype(bf16)

    @pl.when(i >= N_PRB)
    def _():
        for g in range(G_A):
            cols = slice(g * DG_A, (g + 1) * DG_A)
            vs = [v_ref[s * SB:(s + 1) * SB, cols] for s in range(DEC_SEQ)]
            for t in range(DEC_SEQ):
                acc = vs[0] * w8_ref[g * 64 + t * 8]
                for s in range(1, t + 1):
                    acc = acc + vs[s] * w8_ref[g * 64 + t * 8 + s]
                acc = acc + b8_ref[g * 8 + t]
                ya_ref[t * SB:(t + 1) * SB, cols] = (u_ref[t * SB:(t + 1) * SB, cols].astype(f32) * acc).astype(bf16)

    o_ref[...] = ga_ref[...].astype(f32) * _dot(ya_ref[...], wbf_ref[...])


def _gmlp(zu, zv, zg, ws, bs, w_ba, layer):
    bias_full = jnp.repeat(bs.T, DG_A, axis=1)
    w8 = ws[:, :DEC_SEQ, :DEC_SEQ].reshape(G_A * DEC_SEQ * DEC_SEQ)
    b8 = bs[:, :DEC_SEQ].reshape(G_A * DEC_SEQ)
    return pl.pallas_call(
        _gmlp_kernel,
        out_shape=jax.ShapeDtypeStruct((T, D), f32),
        grid=(N_RB,),
        in_specs=[
            pl.BlockSpec((TR, D_A), lambda i: (i, 0)),
            pl.BlockSpec((TR, D_A), lambda i: (i, 0)),
            pl.BlockSpec((TR, D), lambda i: (i, 0)),
            pl.BlockSpec((G_A, CHUNK, CHUNK), lambda i: (0, 0, 0)),
            pl.BlockSpec((CHUNK, D_A), lambda i: (0, 0)),
            pl.BlockSpec(memory_space=pltpu.SMEM),
            pl.BlockSpec(memory_space=pltpu.SMEM),
            pl.BlockSpec((None, D_A, D), lambda i: (layer, 0, 0), pipeline_mode=pl.Buffered(1)),
        ],
        out_specs=pl.BlockSpec((TR, D), lambda i: (i, 0)),
        scratch_shapes=[pltpu.VMEM((G_A, CHUNK, CHUNK), bf16), pltpu.VMEM((TR, D_A), bf16),
                        pltpu.VMEM((D_A, D), bf16)],
        compiler_params=_cparams(("arbitrary",)),
        name=f"gmlp{layer}",
    )(zu, zv, zg, ws, bias_full, w8, b8, w_ba)


def _bcast_row(x8, r):
    return jnp.broadcast_to(x8[r:r + 1, :], x8.shape)


def _cat_rows(tiles):
    return tiles[0] if len(tiles) == 1 else jnp.concatenate(tiles, axis=0)


def _hgrn_chunk(q, g, v, sts, lev, r8, nh):
    c = HG_C
    nv = c // 8
    w = nh * DK
    hs = [slice(i * DK, (i + 1) * DK) for i in range(nh)]
    gl = g * LOG2E
    x_t = []
    for j in range(nv):
        x = gl[8 * j:8 * j + 8, :]
        for sh in (1, 2, 4):
            x = x + jnp.where(r8 >= sh, pltpu.roll(x, sh, 0), 0.0)
        x_t.append(x)
    last = [_bcast_row(x, 7) for x in x_t]
    sh = 1
    while sh < nv:
        last = [last[j] + last[j - sh] if j >= sh else last[j] for j in range(nv)]
        sh *= 2
    a_t = [x_t[0]] + [x_t[j] + last[j - 1] for j in range(1, nv)]
    f = jnp.exp2(gl)
    kk = 1.0 - f
    q_t = [q[8 * j:8 * j + 8, :] for j in range(nv)]
    k_t = [kk[8 * j:8 * j + 8, :] for j in range(nv)]
    a = _cat_rows(a_t)
    odd = (lax.broadcasted_iota(jnp.int32, (c, w), 0) & 1) == 1

    qb = q.astype(bf16)
    kb0 = kk.astype(bf16)
    sc = [jnp.where(lev == 0, _dot_nt(qb[:, hh], kb0[:, hh]), 0.0) for hh in hs]
    for h in HG_MASK_LEVELS:
        if h == 8:
            e = _cat_rows([(a_t[j] - last[j - 1]) if j % 2 else (last[j] - a_t[j]) for j in range(nv)])
        elif h == 4:
            d = _cat_rows([t - _bcast_row(t, 3) for t in a_t])
            e = jnp.minimum(d, -d)
        elif h == 2:
            d = _cat_rows([t - jnp.where(r8 < 4, _bcast_row(t, 1), _bcast_row(t, 5)) for t in a_t])
            e = jnp.minimum(d, -d)
        if h >= 2:
            e = jnp.exp2(e)
        else:
            e = jnp.where(odd, f, 1.0)
        qe = (q * e).astype(bf16)
        ke = (kk * e).astype(bf16)
        sc = [jnp.where(lev == h, _dot_nt(qe[:, hh], ke[:, hh]), sc[i]) for i, hh in enumerate(hs)]
    sc_t = [[s_[8 * j:8 * j + 8, :] for j in range(nv)] for s_ in sc]
    zero8 = jnp.zeros((8, w), f32)
    for h in HG_BLOCK_LEVELS:
        ht = h // 8
        for blk in range(c // (2 * h)):
            lo = range(blk * 2 * ht, blk * 2 * ht + ht)
            up = range(blk * 2 * ht + ht, (blk + 1) * 2 * ht)
            ref = last[blk * 2 * ht + ht - 1]
            qt = _cat_rows([q_t[j] * jnp.exp2(a_t[j] - ref) for j in up]).astype(bf16)
            kz = _cat_rows([k_t[j] * jnp.exp2(ref - a_t[j]) if j in lo else zero8 for j in range(nv)]).astype(bf16)
            for i, hh in enumerate(hs):
                s_ = _dot_nt(qt[:, hh], kz[:, hh])
                for r, j in enumerate(up):
                    sc_t[i][j] = sc_t[i][j] + s_[8 * r:8 * r + 8, :]
    qa = (q * jnp.exp2(a)).astype(bf16)
    a_last = last[-1]
    kb = _cat_rows([k_t[j] * jnp.exp2(a_last - a_t[j]) for j in range(nv)]).astype(bf16)
    fl = jnp.exp2(a_last[0:1, :])
    o, st_new = [], []
    for i, hh in enumerate(hs):
        o.append(_dot_nt(qa[:, hh], sts[i].astype(bf16)) + _dot(_cat_rows(sc_t[i]).astype(bf16), v[:, hh]))
        st_new.append(fl[:, hh] * sts[i] + _dot_tn(v[:, hh], kb[:, hh]))
    return o, st_new


def _hgrn_prompt_kernel(q_ref, g_ref, v_ref, og_ref, lev_ref, gain_ref, on_ref, so_ref, st_ref):
    r8 = lax.broadcasted_iota(jnp.int32, (8, HG_NH * DK), 0)
    for hh in range(HG_NH):
        st_ref[hh] = jnp.zeros((DV, DK), f32)

    def chunk(c, carry):
        r = pl.ds(pl.multiple_of(c * HG_C, HG_C), HG_C)
        o, st_new = _hgrn_chunk(q_ref[r, :].astype(f32), g_ref[r, :], v_ref[r, :],
                                [st_ref[hh] for hh in range(HG_NH)], lev_ref[...], r8, HG_NH)
        for hh in range(HG_NH):
            cols = slice(hh * DV, (hh + 1) * DV)
            on_ref[r, cols] = (_rms(o[hh]) * gain_ref[:, cols] * og_ref[r, cols].astype(f32)).astype(bf16)
            st_ref[hh] = st_new[hh]
        return carry

    lax.fori_loop(0, SEQ // HG_C, chunk, 0)
    for hh in range(HG_NH):
        so_ref[hh] = st_ref[hh].T


def _hgrn_prompt(zqo, zf, zi, gain, layer):
    ng = H_B // HG_NH
    spec = lambda off: pl.BlockSpec((SEQ, HG_NH * DK), lambda b, h: (b, off + h))
    return pl.pallas_call(
        _hgrn_prompt_kernel,
        out_shape=(jax.ShapeDtypeStruct((T_P, D), bf16), jax.ShapeDtypeStruct((BATCH, H_B, DK, DV), f32)),
        grid=(BATCH, ng),
        in_specs=[
            spec(0), spec(0), spec(0), spec(ng),
            pl.BlockSpec((HG_C, HG_C), lambda b, h: (0, 0)),
            pl.BlockSpec((None, 1, HG_NH * DV), lambda b, h: (layer, 0, h)),
        ],
        out_specs=(pl.BlockSpec((SEQ, HG_NH * DV), lambda b, h: (b, h)),
                   pl.BlockSpec((None, HG_NH, DK, DV), lambda b, h: (b, h, 0, 0))),
        scratch_shapes=[pltpu.VMEM((HG_NH, DV, DK), f32)],
        compiler_params=_cparams(("arbitrary", "arbitrary")),
        name=f"hgrn_prompt{layer}",
    )(zqo, zf, zi, zqo, jnp.asarray(_LEV_NP), gain.reshape(DEPTH, 1, D))


def _hgrn_sample_kernel(*refs):
    q_ref, g_ref, v_ref, og_ref, s_ref, gain_ref = refs[:6]
    on_ref, so_ref, qa_scr, kb_scr, v_scr, fs_scr, o_scr = refs[-7:]
    slab = lambda t: slice(t * SB, (t + 1) * SB)
    a, q, kk, v = [], [], [], []
    for t in range(DEC_SEQ):
        gl = g_ref[slab(t), :] * LOG2E
        a.append(gl if t == 0 else a[t - 1] + gl)
        kk.append(1.0 - jnp.exp2(gl))
        q.append(q_ref[slab(t), :].astype(f32))
        v.append(v_ref[slab(t), :].astype(f32))
    a_last = a[DEC_SEQ - 1]
    for t in range(DEC_SEQ):
        qa_scr[slab(t), :] = q[t] * jnp.exp2(a[t])
        kb_scr[slab(t), :] = kk[t] * jnp.exp2(a_last - a[t])
        v_scr[slab(t), :] = v[t]
        o = jnp.sum(q[t] * kk[t], axis=-1, keepdims=True) * v[t]
        for s in range(t):
            o = o + jnp.sum(q[t] * kk[s] * jnp.exp2(a[t] - a[s]), axis=-1, keepdims=True) * v[s]
        o_scr[slab(t), :] = o
    fl = jnp.exp2(a_last)
    f1 = fl.astype(bf16).astype(f32)
    r1 = fl - f1
    f2 = r1.astype(bf16).astype(f32)
    fs_scr[slab(0), :] = f1
    fs_scr[slab(1), :] = f2
    fs_scr[slab(2), :] = r1 - f2
    fs_scr[3 * SB:, :] = jnp.zeros((TR - 3 * SB, DK), f32)

    ones3 = (lax.broadcasted_iota(jnp.int32, (DEC_SEQ, DV), 0) < 3).astype(bf16)
    zero8 = jnp.zeros((DEC_SEQ, DV), bf16)

    def per_batch(b, carry):
        rows = pl.ds(b, DEC_SEQ, stride=SB)
        s0 = s_ref[b]
        o_scr[rows, :] = o_scr[rows, :] + _dot(qa_scr[rows, :].astype(bf16), s0.astype(bf16))
        lhs = jnp.concatenate([kb_scr[rows, :], fs_scr[rows, :]], axis=0).astype(bf16)
        v8 = v_scr[rows, :].astype(bf16)
        rhs = jnp.concatenate([jnp.concatenate([v8, zero8], axis=1), jnp.concatenate([zero8, ones3], axis=1)], axis=0)
        r = _dot_tn(lhs, rhs)
        so_ref[b] = r[:, DV:] * s0 + r[:, :DV]
        return carry

    lax.fori_loop(0, SB, per_batch, 0, unroll=HS_UNROLL)
    gain = gain_ref[...]
    for t in range(DEC_SEQ):
        on_ref[slab(t), :] = (_rms(o_scr[slab(t), :]) * gain * og_ref[slab(t), :].astype(f32)).astype(bf16)


def _hgrn_sample(zqo, zf, zi, state_hgrn, hs_prev, gain, layer):
    rb0 = T_P // TR
    spec = lambda off: pl.BlockSpec((TR, DK), lambda g, h: (rb0 + g, off + h))
    sspec = pl.BlockSpec((None, SB, None, DK, DV), lambda g, h: (layer, g, h, 0, 0))
    ins = [zqo, zf, zi, zqo, state_hgrn, gain.reshape(DEPTH, 1, D)]
    in_specs = [spec(0), spec(0), spec(0), spec(H_B), sspec,
                pl.BlockSpec((None, 1, DV), lambda g, h: (layer, 0, h))]
    aliases = {}
    if hs_prev is not None:
        ins.append(hs_prev)
        in_specs.append(pl.BlockSpec(memory_space=pl.ANY))
        aliases[6] = 1
    return pl.pallas_call(
        _hgrn_sample_kernel,
        out_shape=(jax.ShapeDtypeStruct((T_S, D), bf16), jax.ShapeDtypeStruct(state_hgrn.shape, f32)),
        grid=(N_SG, H_B),
        in_specs=in_specs,
        out_specs=(pl.BlockSpec((TR, DV), lambda g, h: (g, h)), sspec),
        scratch_shapes=[pltpu.VMEM((TR, DV), f32)] * 5,
        input_output_aliases=aliases,
        compiler_params=_cparams(("arbitrary", "arbitrary")),
        name=f"hgrn_sample{layer}",
    )(*ins)


def _mix_kernel(onp_ref, ons_ref, w_ref, a_ref, gb_ref, o_ref, wbf_ref):
    m = pl.program_id(1)

    @pl.when(m == 0)
    def _():
        wbf_ref[...] = w_ref[...].astype(bf16)

    on = jnp.where(m < N_PMB, onp_ref[...], ons_ref[...])
    o_ref[...] = (a_ref[...] + gb_ref[...].astype(f32) * _dot(on, wbf_ref[...])).astype(bf16)


def _branch_mix(on_p, on_s, w_bb, a_gated, zg, layer):
    return pl.pallas_call(
        _mix_kernel,
        out_shape=jax.ShapeDtypeStruct((T, D), bf16),
        grid=(D // TN, N_MB),
        in_specs=[
            pl.BlockSpec((TM, D), lambda n, m: (jnp.minimum(m, N_PMB - 1), 0)),
            pl.BlockSpec((TM, D), lambda n, m: (jnp.maximum(m - N_PMB, 0), 0), pipeline_mode=pl.Buffered(1)),
            pl.BlockSpec((None, D, TN), lambda n, m: (layer, 0, n), pipeline_mode=pl.Buffered(1)),
            pl.BlockSpec((TM, TN), lambda n, m: (m, n)),
            pl.BlockSpec((TM, TN), lambda n, m: (m, D // TN + n)),
        ],
        out_specs=pl.BlockSpec((TM, TN), lambda n, m: (m, n)),
        scratch_shapes=[pltpu.VMEM((D, TN), bf16)],
        compiler_params=_cparams(("arbitrary", "arbitrary")),
        name=f"branch_mix{layer}",
    )(on_p, on_s, w_bb, a_gated, zg)


def _outproj_kernel(y_ref, w_ref, *rest, split_x):
    *x_refs, n2_ref, g1_ref, sh_ref, sc_ref, xo_ref, h_ref = rest
    xo_ref[...] = _dot(y_ref[...], w_ref[...])
    n2 = n2_ref[...]
    g1 = g1_ref[...]
    sh = sh_ref[...]
    sc1 = 1.0 + sc_ref[...]
    is_prompt = pl.program_id(0) < N_PRB
    for s in range(TR // SB):
        rows = slice(s * SB, (s + 1) * SB)
        x = jnp.where(is_prompt, x_refs[0][rows, :], x_refs[1][rows, :]) if split_x else x_refs[0][rows, :]
        xm = x + g1 * xo_ref[rows, :]
        xo_ref[rows, :] = xm
        h_ref[rows, :] = (_rms(xm) * n2 * sc1 + sh).astype(bf16)


def _outproj(y, w_out_bf, xs, n2, mods, layer):
    split_x = len(xs) == 2
    return pl.pallas_call(
        functools.partial(_outproj_kernel, split_x=split_x),
        out_shape=(jax.ShapeDtypeStruct((T, D), f32), jax.ShapeDtypeStruct((T, D), bf16)),
        grid=(N_RB,),
        in_specs=[
            pl.BlockSpec((TR, D), lambda i: (i, 0)),
            pl.BlockSpec((None, D, D), lambda i: (layer, 0, 0), pipeline_mode=pl.Buffered(1)),
            *(_x_specs(TR) if split_x else [pl.BlockSpec((TR, D), lambda i: (i, 0))]),
            pl.BlockSpec((None, 1, D), lambda i: (layer, 0, 0)),
            _mod_spec(layer, _MOD_G1), _mod_spec(layer, _MOD_SH2), _mod_spec(layer, _MOD_SC2),
        ],
        out_specs=(pl.BlockSpec((TR, D), lambda i: (i, 0)), pl.BlockSpec((TR, D), lambda i: (i, 0))),
        compiler_params=_cparams(("arbitrary",)),
        name=f"outproj{layer}",
    )(y, w_out_bf, *xs, n2.reshape(DEPTH, 1, D), mods, mods, mods)


def _conv3(cur, m1, m2, cw_ref, cb_ref):
    return cb_ref[...] + cw_ref[0:1, :] * m2 + cw_ref[1:2, :] * m1 + cw_ref[2:3, :] * cur


def _up_kernel(h_ref, wa_ref, wb_ref, cwa_ref, cwb_ref, cba_ref, cbb_ref, sa_ref, sb_ref,
               act_ref, ta_ref, tb_ref, wabf_ref, wbbf_ref, haloa_ref, halob_ref):
    m = pl.program_id(1)

    @pl.when(m == 0)
    def _():
        wabf_ref[...] = wa_ref[...].astype(bf16)
        wbbf_ref[...] = wb_ref[...].astype(bf16)
        ta_ref[...] = jnp.zeros_like(ta_ref)
        tb_ref[...] = jnp.zeros_like(tb_ref)
        haloa_ref[...] = jnp.zeros_like(haloa_ref)
        halob_ref[...] = jnp.zeros_like(halob_ref)

    h = h_ref[...]
    acc_a = _dot(h, wabf_ref[...])
    acc_b = _dot(h, wbbf_ref[...])
    seq_start = (m % MB_PER_SEQ) == 0
    r8 = lax.broadcasted_iota(jnp.int32, (8, TC_FF), 0)

    def conv(cur, halo_ref, cw_ref, cb_ref):
        body = _conv3(cur, pltpu.roll(cur, 1, 0), pltpu.roll(cur, 2, 0), cw_ref, cb_ref)
        halo = jnp.where(seq_start, 0.0, halo_ref[...])
        cur8 = cur[0:8, :]
        m1 = jnp.where(r8 < 1, pltpu.roll(halo, 1, 0), pltpu.roll(cur8, 1, 0))
        m2 = jnp.where(r8 < 2, pltpu.roll(halo, 2, 0), pltpu.roll(cur8, 2, 0))
        return body, _conv3(cur8, m1, m2, cw_ref, cb_ref)

    ca, ca8 = conv(acc_a, haloa_ref, cwa_ref, cba_ref)
    cb, cb8 = conv(acc_b, halob_ref, cwb_ref, cbb_ref)
    act_ref[...] = (jax.nn.gelu(ca) * cb).astype(bf16)
    act_ref[0:8, :] = (jax.nn.gelu(ca8) * cb8).astype(bf16)
    ta8 = acc_a[TM - 8:TM, :]
    tb8 = acc_b[TM - 8:TM, :]
    haloa_ref[...] = ta8
    halob_ref[...] = tb8

    @pl.when((m < N_PMB) & (m % MB_PER_SEQ == MB_PER_SEQ - 1))
    def _():
        r = pl.ds(pl.multiple_of((m // MB_PER_SEQ) * 8, 8), 8)
        ta_ref[r, :] = ta8
        tb_ref[r, :] = tb8

    @pl.when(m == N_PMB)
    def _():
        def conv_s(acc, st_ref, cw_ref, cb_ref, g, t):
            slab = lambda j: acc[(g * DEC_SEQ + j) * SB:(g * DEC_SEQ + j + 1) * SB, :]
            m1 = slab(t - 1) if t >= 1 else st_ref[1, g]
            m2 = slab(t - 2) if t >= 2 else st_ref[t, g]
            return _conv3(slab(t), m1, m2, cw_ref, cb_ref)

        for g in range(N_SG):
            for t in range(DEC_SEQ):
                ca_ = conv_s(acc_a, sa_ref, cwa_ref, cba_ref, g, t)
                cb_ = conv_s(acc_b, sb_ref, cwb_ref, cbb_ref, g, t)
                r0 = (g * DEC_SEQ + t) * SB
                act_ref[r0:r0 + SB, :] = (jax.nn.gelu(ca_) * cb_).astype(bf16)
            for j in range(CONV_W - 1):
                r0 = (g * DEC_SEQ + DEC_SEQ - (CONV_W - 1) + j) * SB
                d0 = 8 * 16 + (g * (CONV_W - 1) + j) * SB
                ta_ref[d0:d0 + SB, :] = acc_a[r0:r0 + SB, :]
                tb_ref[d0:d0 + SB, :] = acc_b[r0:r0 + SB, :]


def _up_act(h, w_up, conv_w, conv_b, conv_state, layer):
    nb = N_CFF
    cws = lambda off: pl.BlockSpec((None, CONV_W, TC_FF), lambda n, m: (layer, 0, off + n))
    cbs = lambda off: pl.BlockSpec((None, 1, TC_FF), lambda n, m: (layer, 0, off + n))
    sts = lambda off: pl.BlockSpec((CONV_W - 1, N_SG, SB, TC_FF), lambda n, m: (0, 0, 0, off + n))
    ws = lambda off: pl.BlockSpec((None, D, TC_FF), lambda n, m: (layer, 0, off + n))
    cb3 = conv_b.reshape(DEPTH, 1, 2 * D_FF)
    return pl.pallas_call(
        _up_kernel,
        out_shape=(jax.ShapeDtypeStruct((T, D_FF), bf16), jax.ShapeDtypeStruct((TAIL_ROWS, D_FF), f32),
                   jax.ShapeDtypeStruct((TAIL_ROWS, D_FF), f32)),
        grid=(nb, N_MB),
        in_specs=[pl.BlockSpec((TM, D), lambda n, m: (m, 0)), ws(0), ws(nb), cws(0), cws(nb), cbs(0), cbs(nb),
                  sts(0), sts(nb)],
        out_specs=(pl.BlockSpec((TM, TC_FF), lambda n, m: (m, n)),
                   pl.BlockSpec((TAIL_ROWS, TC_FF), lambda n, m: (0, n)),
                   pl.BlockSpec((TAIL_ROWS, TC_FF), lambda n, m: (0, n))),
        scratch_shapes=[pltpu.VMEM((D, TC_FF), bf16), pltpu.VMEM((D, TC_FF), bf16),
                        pltpu.VMEM((8, TC_FF), f32), pltpu.VMEM((8, TC_FF), f32)],
        compiler_params=_cparams(("arbitrary", "arbitrary")),
        name=f"up_act{layer}",
    )(h, w_up, w_up, conv_w, conv_w, cb3, cb3, conv_state, conv_state)


def _ffn_kernel(act_ref, wd_ref, xm_ref, g2_ref, gain_ref, sh_ref, sc_ref, xo_ref, ho_ref):
    xo_ref[...] = _dot(act_ref[...], wd_ref[...])
    g2 = g2_ref[...]
    gain = gain_ref[...]
    for s in range(TR_FF // SB):
        rows = slice(s * SB, (s + 1) * SB)
        xn = xm_ref[rows, :] + g2 * xo_ref[rows, :]
        xo_ref[rows, :] = xn
        ho_ref[rows, :] = (_rms(xn) * gain * (1.0 + sc_ref[...]) + sh_ref[...]).astype(bf16)


def _ffn_final_kernel(act_ref, wd_ref, xm_ref, g2_ref, gain_ref, yp_ref, ys_ref, y_scr):
    y_scr[...] = _dot(act_ref[...], wd_ref[...])
    g2 = g2_ref[...]
    gain = gain_ref[...]
    for s in range(TR_FF // SB):
        rows = slice(s * SB, (s + 1) * SB)
        y_scr[rows, :] = _rms(xm_ref[rows, :] + g2 * y_scr[rows, :]) * gain
    is_prompt = pl.program_id(0) < T_P // TR_FF

    @pl.when(is_prompt)
    def _():
        yp_ref[...] = y_scr[...]

    @pl.when(jnp.logical_not(is_prompt))
    def _():
        ys_ref[...] = y_scr[...]


def _ffn(act, wd_bf, x_mid, mods, gain, layer, final):
    row = lambda i: (i, 0)
    n_p = T_P // TR_FF
    in_specs = [
        pl.BlockSpec((TR_FF, D_FF), row),
        pl.BlockSpec((None, D_FF, D), lambda i: (layer, 0, 0), pipeline_mode=pl.Buffered(1)),
        pl.BlockSpec((TR_FF, D), row),
        _mod_spec(layer, _MOD_G2, TR_FF),
        pl.BlockSpec((1, D), lambda i: (0, 0)),
    ]
    if final:
        return pl.pallas_call(
            _ffn_final_kernel,
            out_shape=(jax.ShapeDtypeStruct((T_P, D), f32), jax.ShapeDtypeStruct((T_S, D), f32)),
            grid=(T // TR_FF,),
            in_specs=in_specs,
            out_specs=(pl.BlockSpec((TR_FF, D), lambda i: (jnp.minimum(i, n_p - 1), 0)),
                       pl.BlockSpec((TR_FF, D), lambda i: (jnp.maximum(i - n_p, 0), 0))),
            scratch_shapes=[pltpu.VMEM((TR_FF, D), f32)],
            compiler_params=_cparams(("arbitrary",)),
            name=f"ffn{layer}",
        )(act, wd_bf, x_mid, mods, gain.reshape(1, D))
    return pl.pallas_call(
        _ffn_kernel,
        out_shape=(jax.ShapeDtypeStruct((T, D), f32), jax.ShapeDtypeStruct((T, D), bf16)),
        grid=(T // TR_FF,),
        in_specs=in_specs + [_mod_spec(layer + 1, _MOD_SH1, TR_FF), _mod_spec(layer + 1, _MOD_SC1, TR_FF)],
        out_specs=(pl.BlockSpec((TR_FF, D), row), pl.BlockSpec((TR_FF, D), row)),
        compiler_params=_cparams(("arbitrary",)),
        name=f"ffn{layer}",
    )(act, wd_bf, x_mid, mods, gain.reshape(1, D), mods, mods)


def _to_rows(a):
    f = a.shape[-1]
    return a.reshape(N_SG, SB, DEC_SEQ, f).transpose(0, 2, 1, 3).reshape(T_S, f)


def _from_rows(a, steps=DEC_SEQ):
    f = a.shape[-1]
    return a.reshape(N_SG, steps, SB, f).transpose(0, 2, 1, 3).reshape(DEC_BATCH, steps, f)


def kernel(x_prompt, x_sample, c_prompt, c_sample, state_hgrn, state_ffn_conv, ada_w, ada_b, norm1_g, norm2_g,
           w_in, gmlp_v_g, gmlp_ws, gmlp_bs, w_branch_a, hgrn_lower_bounds, hgrn_norm_g, w_branch_b, w_out,
           w_up, conv_w, conv_b, w_down, final_norm_g):
    xs = (x_prompt.reshape(T_P, D), _to_rows(x_sample))
    c_all = jnp.concatenate([c_sample, c_prompt, jnp.zeros((4, D), f32)], axis=0)
    mods = _ada(c_all, ada_w, ada_b)

    wout_bf = w_out.astype(bf16)
    wd_bf = w_down.astype(bf16)
    conv_st = state_ffn_conv.transpose(0, 2, 1, 3).reshape(DEPTH, CONV_W - 1, N_SG, SB, 2 * D_FF)

    h = _norm_mod(*xs, norm1_g[0], mods, 0)
    hs_all = None
    hp, cp, cs, vp, vs = [], [], [], [], []
    y_p = y_s = None
    for l in range(DEPTH):
        zu, zv, zqo, zf, zi, zg = _inproj(h, w_in, hgrn_lower_bounds, gmlp_v_g, l)
        a_gated = _gmlp(zu, zv, zg, gmlp_ws[l], gmlp_bs[l], w_branch_a, l)
        on_p, s_p = _hgrn_prompt(zqo, zf, zi, hgrn_norm_g, l)
        on_s, hs_all = _hgrn_sample(zqo, zf, zi, state_hgrn, hs_all, hgrn_norm_g, l)
        ymix = _branch_mix(on_p, on_s, w_branch_b, a_gated, zg, l)
        x_mid, h2 = _outproj(ymix, wout_bf, xs, norm2_g, mods, l)
        act, tail_a, tail_b = _up_act(h2, w_up, conv_w, conv_b, conv_st[l], l)
        if l == DEPTH - 1:
            y_p, y_s = _ffn(act, wd_bf, x_mid, mods, final_norm_g, l, True)
        else:
            x, h = _ffn(act, wd_bf, x_mid, mods, norm1_g[l + 1], l, False)
            xs = (x,)
        tail = jnp.concatenate([tail_a, tail_b], axis=1)
        hp.append(s_p)
        cp.append(tail[:8 * BATCH].reshape(BATCH, 8, 2 * D_FF)[:, 8 - (CONV_W - 1):])
        cs.append(_from_rows(tail[8 * 16:], CONV_W - 1))
        vp.append(jnp.stack([zv[(b + 1) * SEQ - CHUNK:(b + 1) * SEQ] for b in range(BATCH)]))
        vs.append(_from_rows(zv[T_P:]))

    y_prompt = y_p.reshape(BATCH, SEQ, D)
    y_sample = _from_rows(y_s)
    return (y_prompt, y_sample, jnp.stack(hp), hs_all, jnp.stack(cp), jnp.stack(cs), jnp.stack(vp), jnp.stack(vs))
```

```python
import functools

import numpy as np
import jax
import jax.numpy as jnp
from jax import lax
from jax.experimental import pallas as pl
from jax.experimental.pallas import tpu as pltpu

f32 = jnp.float32
bf16 = jnp.bfloat16

D = 2048
BATCH = 4
SEQ = 2048
DEPTH = 2
DEC_BATCH = 128
DEC_SEQ = 8
G_A = 8
DG_A = 128
D_A = G_A * DG_A
CHUNK = 128
H_B = 16
DK = 128
DV = 128
D_FF = 5632
CONV_W = 3
N_IN = 2 * D_A + 6 * D
EPS = 1e-6

T_P = BATCH * SEQ
T_S = DEC_BATCH * DEC_SEQ
T = T_P + T_S
SB = 64
N_SG = DEC_BATCH // SB
TR = DEC_SEQ * SB
N_RB = T // TR
N_PRB = T_P // TR
RB_PER_SEQ = SEQ // TR
TM = 1024
N_MB = T // TM
N_PMB = T_P // TM
MB_PER_SEQ = SEQ // TM
TN = 1024
TC_FF = 512
N_CFF = D_FF // TC_FF
TR_FF = 256
HG_C = 128
HG_NH = 8
HG_BLOCK_LEVELS = (64, 32, 16)
HG_MASK_LEVELS = (8, 4, 2, 1)
HS_UNROLL = 16
LOG2E = 1.4426950408889634
TAIL_ROWS = 8 * 16 + (CONV_W - 1) * DEC_BATCH

_U0, _V0, _Q0, _F0, _I0, _OG0, _GA0 = 0, 1, 2, 4, 6, 8, 10
_MOD_SH1, _MOD_SC1, _MOD_G1, _MOD_SH2, _MOD_SC2, _MOD_G2 = range(6)
N_MODJ = BATCH + N_SG


def _cparams(sem):
    return pltpu.CompilerParams(dimension_semantics=sem)


def _hgrn_level_table(c):
    t = np.arange(c)
    lev = np.full((c, c), -1, np.int32)
    lev[t, t] = 0
    for h in HG_MASK_LEVELS:
        upper = (t % (2 * h)) >= h
        same = (t[:, None] // (2 * h)) == (t[None, :] // (2 * h))
        lev[same & upper[:, None] & ~upper[None, :]] = h
    return lev


_LEV_NP = _hgrn_level_table(HG_C)


def _rms(x, eps=EPS):
    return x * lax.rsqrt(jnp.mean(x * x, axis=-1, keepdims=True) + eps)


def _dot(a, b):
    return jnp.dot(a, b, preferred_element_type=f32)


def _dot_nt(a, b):
    return lax.dot_general(a, b, (((1,), (1,)), ((), ())), preferred_element_type=f32)


def _dot_tn(a, b):
    return lax.dot_general(a, b, (((0,), (0,)), ((), ())), preferred_element_type=f32)


def _modj(i, rows):
    return jnp.where(i < T_P // rows, i // (SEQ // rows), BATCH + (i - T_P // rows) // (TR // rows))


def _mod_spec(layer, kind, rows=TR):
    return pl.BlockSpec((None, None, None, SB, D), lambda i, *_: (layer, kind, _modj(i, rows), 0, 0))


def _ada_kernel(c_ref, w_ref, b_ref, o_ref):
    c = c_ref[...]
    a = (c * jax.nn.sigmoid(c)).astype(bf16)
    r = _dot(a, w_ref[...].astype(bf16)) + b_ref[...]
    for g in range(N_SG):
        o_ref[BATCH + g] = r[g * SB:(g + 1) * SB]
    for j in range(BATCH):
        o_ref[j] = jnp.broadcast_to(r[DEC_BATCH + j:DEC_BATCH + j + 1], (SB, TN))


def _ada(c_all, ada_w, ada_b):
    nrow = c_all.shape[0]
    ncol = 6 * D // TN
    per_kind = D // TN
    return pl.pallas_call(
        _ada_kernel,
        out_shape=jax.ShapeDtypeStruct((DEPTH, 6, N_MODJ, SB, D), f32),
        grid=(DEPTH, ncol),
        in_specs=[
            pl.BlockSpec((nrow, D), lambda l, n: (0, 0)),
            pl.BlockSpec((None, D, TN), lambda l, n: (l, 0, n)),
            pl.BlockSpec((None, 1, TN), lambda l, n: (l, 0, n)),
        ],
        out_specs=pl.BlockSpec((None, None, N_MODJ, SB, TN), lambda l, n: (l, n // per_kind, 0, 0, n % per_kind)),
        compiler_params=_cparams(("arbitrary", "arbitrary")),
        name="ada_mod",
    )(c_all, ada_w, ada_b.reshape(DEPTH, 1, 6 * D))


def _x_specs(rows):
    n_p = T_P // rows
    return [pl.BlockSpec((rows, D), lambda i: (jnp.minimum(i, n_p - 1), 0)),
            pl.BlockSpec((rows, D), lambda i: (jnp.maximum(i - n_p, 0), 0))]


def _norm_mod_kernel(xp_ref, xs_ref, g_ref, sh_ref, sc_ref, o_ref):
    g = g_ref[...]
    sh = sh_ref[...]
    sc1 = 1.0 + sc_ref[...]
    is_prompt = pl.program_id(0) < N_PRB
    for s in range(TR // SB):
        rows = slice(s * SB, (s + 1) * SB)
        x = jnp.where(is_prompt, xp_ref[rows, :], xs_ref[rows, :])
        o_ref[rows, :] = (_rms(x) * g * sc1 + sh).astype(bf16)


def _norm_mod(xp, xs, gain, mods, layer):
    return pl.pallas_call(
        _norm_mod_kernel,
        out_shape=jax.ShapeDtypeStruct((T, D), bf16),
        grid=(N_RB,),
        in_specs=[
            *_x_specs(TR),
            pl.BlockSpec((1, D), lambda i: (0, 0)),
            _mod_spec(layer, _MOD_SH1),
            _mod_spec(layer, _MOD_SC1),
        ],
        out_specs=pl.BlockSpec((TR, D), lambda i: (i, 0)),
        compiler_params=_cparams(("arbitrary",)),
        name="norm_mod",
    )(xp, xs, gain.reshape(1, D), mods, mods)


def _proj_kernel(h_ref, w_ref, *rest, epilogue):
    *extra, o_ref, wbf_ref = rest

    @pl.when(pl.program_id(1) == 0)
    def _():
        wbf_ref[...] = w_ref[...].astype(bf16)

    epilogue(_dot(h_ref[...], wbf_ref[...]), o_ref, *extra)


def _epi_gelu(acc, o_ref):
    o_ref[...] = jax.nn.gelu(acc).astype(o_ref.dtype)


def _epi_v(acc, o_ref, vg_ref):
    v = jax.nn.gelu(acc)
    for g in range(G_A):
        cols = slice(g * DG_A, (g + 1) * DG_A)
        o_ref[:, cols] = _rms(v[:, cols]) * vg_ref[:, cols]


def _epi_silu(acc, o_ref, scale_ref):
    o_ref[...] = (acc * jax.nn.sigmoid(acc) * scale_ref[pl.program_id(0)]).astype(o_ref.dtype)


def _epi_logf(acc, o_ref, lb_ref, *, layer):
    p = jax.nn.softmax(lb_ref[...], axis=0)
    lb = jnp.zeros((1, TN), f32)
    for j in range(1, layer + 1):
        lb = lb + p[j:j + 1]
    a = jnp.log(lb)
    b = jnp.log1p(-lb) + jnp.minimum(acc, 0.0) - jnp.log(1.0 + jnp.exp(-jnp.abs(acc)))
    o_ref[...] = jnp.maximum(a, b) + jnp.log(1.0 + jnp.exp(-jnp.abs(a - b)))


def _epi_id(acc, o_ref):
    o_ref[...] = acc.astype(o_ref.dtype)


def _epi_sigmoid(acc, o_ref):
    o_ref[...] = jax.nn.sigmoid(acc).astype(o_ref.dtype)


def _proj(name, h, w_in, layer, col_of, n_tiles, out_dtype, epilogue, extra=(), extra_specs=()):
    return pl.pallas_call(
        functools.partial(_proj_kernel, epilogue=epilogue),
        out_shape=jax.ShapeDtypeStruct((T, n_tiles * TN), out_dtype),
        grid=(n_tiles, N_MB),
        in_specs=[
            pl.BlockSpec((TM, D), lambda n, m: (m, 0)),
            pl.BlockSpec((None, D, TN), lambda n, m: (layer, 0, col_of(n))),
            *extra_specs,
        ],
        out_specs=pl.BlockSpec((TM, TN), lambda n, m: (m, n)),
        scratch_shapes=[pltpu.VMEM((D, TN), bf16)],
        compiler_params=_cparams(("arbitrary", "arbitrary")),
        name=f"{name}{layer}",
    )(h, w_in, *extra)


def _inproj(h, w_in, lower_bounds, v_gain, layer):
    smem = pl.BlockSpec(memory_space=pltpu.SMEM)
    zu = _proj("in_u", h, w_in, layer, lambda n: _U0 + n, 1, bf16, _epi_gelu)
    zv = _proj("in_v", h, w_in, layer, lambda n: _V0 + n, 1, f32, _epi_v,
               (v_gain.reshape(DEPTH, 1, D_A),), (pl.BlockSpec((None, 1, D_A), lambda n, m: (layer, 0, 0)),))
    scales = jnp.asarray([DK ** -0.5] * 2 + [1.0] * 2, f32)
    zqo = _proj("in_qo", h, w_in, layer, lambda n: jnp.where(n < 2, _Q0 + n, _OG0 + n - 2), 4, bf16, _epi_silu,
                (scales,), (smem,))
    zf = _proj("in_f", h, w_in, layer, lambda n: _F0 + n, 2, f32, functools.partial(_epi_logf, layer=layer),
               (lower_bounds,), (pl.BlockSpec((DEPTH, TN), lambda n, m: (0, n)),))
    zi = _proj("in_i", h, w_in, layer, lambda n: _I0 + n, 2, bf16, _epi_id)
    zg = _proj("in_g", h, w_in, layer, lambda n: _GA0 + n, 4, bf16, _epi_sigmoid)
    return zu, zv, zqo, zf, zi, zg


def _gmlp_kernel(u_ref, v_ref, ga_ref, ws_ref, bias_ref, w8_ref, b8_ref, wba_ref, o_ref, wm_ref, ya_ref, wbf_ref):
    i = pl.program_id(0)

    @pl.when(i == 0)
    def _():
        wbf_ref[...] = wba_ref[...].astype(bf16)
        r = lax.broadcasted_iota(jnp.int32, (CHUNK, CHUNK), 0)
        c = lax.broadcasted_iota(jnp.int32, (CHUNK, CHUNK), 1)
        for g in range(G_A):
            wm_ref[g] = jnp.where(r >= c, ws_ref[g], 0.0).astype(bf16)

    @pl.when(i < N_PRB)
    def _():
        for ch in range(TR // CHUNK):
            rows = slice(ch * CHUNK, (ch + 1) * CHUNK)
            for g in range(G_A):
                cols = slice(g * DG_A, (g + 1) * DG_A)
                s = _dot(wm_ref[g], v_ref[rows, cols].astype(bf16)) + bias_ref[:, cols]
                ya_ref[rows, cols] = (u_ref[rows, cols].astype(f32) * s).astype(bf16)

    @pl.when(i >= N_PRB)
    def _():
        for g in range(G_A):
            cols = slice(g * DG_A, (g + 1) * DG_A)
            vs = [v_ref[s * SB:(s + 1) * SB, cols] for s in range(DEC_SEQ)]
            for t in range(DEC_SEQ):
                acc = vs[0] * w8_ref[g * 64 + t * 8]
                for s in range(1, t + 1):
                    acc = acc + vs[s] * w8_ref[g * 64 + t * 8 + s]
                acc = acc + b8_ref[g * 8 + t]
                ya_ref[t * SB:(t + 1) * SB, cols] = (u_ref[t * SB:(t + 1) * SB, cols].astype(f32) * acc).astype(bf16)

    o_ref[...] = (ga_ref[...].astype(f32) * _dot(ya_ref[...], wbf_ref[...])).astype(bf16)


def _gmlp(zu, zv, zg, ws, bs, w_ba, layer):
    bias_full = jnp.repeat(bs.T, DG_A, axis=1)
    w8 = ws[:, :DEC_SEQ, :DEC_SEQ].reshape(G_A * DEC_SEQ * DEC_SEQ)
    b8 = bs[:, :DEC_SEQ].reshape(G_A * DEC_SEQ)
    return pl.pallas_call(
        _gmlp_kernel,
        out_shape=jax.ShapeDtypeStruct((T, D), bf16),
        grid=(N_RB,),
        in_specs=[
            pl.BlockSpec((TR, D_A), lambda i: (i, 0)),
            pl.BlockSpec((TR, D_A), lambda i: (i, 0)),
            pl.BlockSpec((TR, D), lambda i: (i, 0)),
            pl.BlockSpec((G_A, CHUNK, CHUNK), lambda i: (0, 0, 0)),
            pl.BlockSpec((CHUNK, D_A), lambda i: (0, 0)),
            pl.BlockSpec(memory_space=pltpu.SMEM),
            pl.BlockSpec(memory_space=pltpu.SMEM),
            pl.BlockSpec((None, D_A, D), lambda i: (layer, 0, 0), pipeline_mode=pl.Buffered(1)),
        ],
        out_specs=pl.BlockSpec((TR, D), lambda i: (i, 0)),
        scratch_shapes=[pltpu.VMEM((G_A, CHUNK, CHUNK), bf16), pltpu.VMEM((TR, D_A), bf16),
                        pltpu.VMEM((D_A, D), bf16)],
        compiler_params=_cparams(("arbitrary",)),
        name=f"gmlp{layer}",
    )(zu, zv, zg, ws, bias_full, w8, b8, w_ba)


def _bcast_row(x8, r):
    return jnp.broadcast_to(x8[r:r + 1, :], x8.shape)


def _cat_rows(tiles):
    return tiles[0] if len(tiles) == 1 else jnp.concatenate(tiles, axis=0)


def _hgrn_chunk(q, g, v, sts, lev, r8, nh):
    c = HG_C
    nv = c // 8
    w = nh * DK
    hs = [slice(i * DK, (i + 1) * DK) for i in range(nh)]
    gl = g * LOG2E
    x_t = []
    for j in range(nv):
        x = gl[8 * j:8 * j + 8, :]
        for sh in (1, 2, 4):
            x = x + jnp.where(r8 >= sh, pltpu.roll(x, sh, 0), 0.0)
        x_t.append(x)
    last = [_bcast_row(x, 7) for x in x_t]
    sh = 1
    while sh < nv:
        last = [last[j] + last[j - sh] if j >= sh else last[j] for j in range(nv)]
        sh *= 2
    a_t = [x_t[0]] + [x_t[j] + last[j - 1] for j in range(1, nv)]
    f = jnp.exp2(gl)
    kk = 1.0 - f
    q_t = [q[8 * j:8 * j + 8, :] for j in range(nv)]
    k_t = [kk[8 * j:8 * j + 8, :] for j in range(nv)]
    a = _cat_rows(a_t)
    odd = (lax.broadcasted_iota(jnp.int32, (c, w), 0) & 1) == 1

    qb = q.astype(bf16)
    kb0 = kk.astype(bf16)
    sc = [jnp.where(lev == 0, _dot_nt(qb[:, hh], kb0[:, hh]), 0.0) for hh in hs]
    for h in HG_MASK_LEVELS:
        if h == 8:
            e = _cat_rows([(a_t[j] - last[j - 1]) if j % 2 else (last[j] - a_t[j]) for j in range(nv)])
        elif h == 4:
            d = _cat_rows([t - _bcast_row(t, 3) for t in a_t])
            e = jnp.minimum(d, -d)
        elif h == 2:
            d = _cat_rows([t - jnp.where(r8 < 4, _bcast_row(t, 1), _bcast_row(t, 5)) for t in a_t])
            e = jnp.minimum(d, -d)
        if h >= 2:
            e = jnp.exp2(e)
        else:
            e = jnp.where(odd, f, 1.0)
        qe = (q * e).astype(bf16)
        ke = (kk * e).astype(bf16)
        sc = [jnp.where(lev == h, _dot_nt(qe[:, hh], ke[:, hh]), sc[i]) for i, hh in enumerate(hs)]
    sc_t = [[s_[8 * j:8 * j + 8, :] for j in range(nv)] for s_ in sc]
    zero8 = jnp.zeros((8, w), f32)
    for h in HG_BLOCK_LEVELS:
        ht = h // 8
        for blk in range(c // (2 * h)):
            lo = range(blk * 2 * ht, blk * 2 * ht + ht)
            up = range(blk * 2 * ht + ht, (blk + 1) * 2 * ht)
            ref = last[blk * 2 * ht + ht - 1]
            qt = _cat_rows([q_t[j] * jnp.exp2(a_t[j] - ref) for j in up]).astype(bf16)
            kz = _cat_rows([k_t[j] * jnp.exp2(ref - a_t[j]) if j in lo else zero8 for j in range(nv)]).astype(bf16)
            for i, hh in enumerate(hs):
                s_ = _dot_nt(qt[:, hh], kz[:, hh])
                for r, j in enumerate(up):
                    sc_t[i][j] = sc_t[i][j] + s_[8 * r:8 * r + 8, :]
    qa = (q * jnp.exp2(a)).astype(bf16)
    a_last = last[-1]
    kb = _cat_rows([k_t[j] * jnp.exp2(a_last - a_t[j]) for j in range(nv)]).astype(bf16)
    fl = jnp.exp2(a_last[0:1, :])
    o, st_new = [], []
    for i, hh in enumerate(hs):
        o.append(_dot_nt(qa[:, hh], sts[i].astype(bf16)) + _dot(_cat_rows(sc_t[i]).astype(bf16), v[:, hh]))
        st_new.append(fl[:, hh] * sts[i] + _dot_tn(v[:, hh], kb[:, hh]))
    return o, st_new


def _hgrn_prompt_kernel(q_ref, g_ref, v_ref, og_ref, lev_ref, gain_ref, on_ref, so_ref, st_ref):
    r8 = lax.broadcasted_iota(jnp.int32, (8, HG_NH * DK), 0)
    for hh in range(HG_NH):
        st_ref[hh] = jnp.zeros((DV, DK), f32)

    def chunk(c, carry):
        r = pl.ds(pl.multiple_of(c * HG_C, HG_C), HG_C)
        o, st_new = _hgrn_chunk(q_ref[r, :].astype(f32), g_ref[r, :], v_ref[r, :],
                                [st_ref[hh] for hh in range(HG_NH)], lev_ref[...], r8, HG_NH)
        for hh in range(HG_NH):
            cols = slice(hh * DV, (hh + 1) * DV)
            on_ref[r, cols] = (_rms(o[hh]) * gain_ref[:, cols] * og_ref[r, cols].astype(f32)).astype(bf16)
            st_ref[hh] = st_new[hh]
        return carry

    lax.fori_loop(0, SEQ // HG_C, chunk, 0)
    for hh in range(HG_NH):
        so_ref[hh] = st_ref[hh].T


def _hgrn_prompt(zqo, zf, zi, gain, layer):
    ng = H_B // HG_NH
    spec = lambda off: pl.BlockSpec((SEQ, HG_NH * DK), lambda b, h: (b, off + h))
    return pl.pallas_call(
        _hgrn_prompt_kernel,
        out_shape=(jax.ShapeDtypeStruct((T_P, D), bf16), jax.ShapeDtypeStruct((BATCH, H_B, DK, DV), f32)),
        grid=(BATCH, ng),
        in_specs=[
            spec(0), spec(0), spec(0), spec(ng),
            pl.BlockSpec((HG_C, HG_C), lambda b, h: (0, 0)),
            pl.BlockSpec((None, 1, HG_NH * DV), lambda b, h: (layer, 0, h)),
        ],
        out_specs=(pl.BlockSpec((SEQ, HG_NH * DV), lambda b, h: (b, h)),
                   pl.BlockSpec((None, HG_NH, DK, DV), lambda b, h: (b, h, 0, 0))),
        scratch_shapes=[pltpu.VMEM((HG_NH, DV, DK), f32)],
        compiler_params=_cparams(("arbitrary", "arbitrary")),
        name=f"hgrn_prompt{layer}",
    )(zqo, zf, zi, zqo, jnp.asarray(_LEV_NP), gain.reshape(DEPTH, 1, D))


def _hgrn_sample_kernel(*refs):
    q_ref, g_ref, v_ref, og_ref, s_ref, gain_ref = refs[:6]
    on_ref, so_ref, qa_scr, kb_scr, v_scr, fs_scr, o_scr = refs[-7:]
    slab = lambda t: slice(t * SB, (t + 1) * SB)
    a, q, kk, v = [], [], [], []
    for t in range(DEC_SEQ):
        gl = g_ref[slab(t), :] * LOG2E
        a.append(gl if t == 0 else a[t - 1] + gl)
        kk.append(1.0 - jnp.exp2(gl))
        q.append(q_ref[slab(t), :].astype(f32))
        v.append(v_ref[slab(t), :].astype(f32))
    a_last = a[DEC_SEQ - 1]
    for t in range(DEC_SEQ):
        qa_scr[slab(t), :] = q[t] * jnp.exp2(a[t])
        kb_scr[slab(t), :] = kk[t] * jnp.exp2(a_last - a[t])
        v_scr[slab(t), :] = v[t]
        o = jnp.sum(q[t] * kk[t], axis=-1, keepdims=True) * v[t]
        for s in range(t):
            o = o + jnp.sum(q[t] * kk[s] * jnp.exp2(a[t] - a[s]), axis=-1, keepdims=True) * v[s]
        o_scr[slab(t), :] = o
    fl = jnp.exp2(a_last)
    f1 = fl.astype(bf16).astype(f32)
    r1 = fl - f1
    f2 = r1.astype(bf16).astype(f32)
    fs_scr[slab(0), :] = f1
    fs_scr[slab(1), :] = f2
    fs_scr[slab(2), :] = r1 - f2
    fs_scr[3 * SB:, :] = jnp.zeros((TR - 3 * SB, DK), f32)

    ones3 = (lax.broadcasted_iota(jnp.int32, (DEC_SEQ, DV), 0) < 3).astype(bf16)
    zero8 = jnp.zeros((DEC_SEQ, DV), bf16)

    def per_batch(b, carry):
        rows = pl.ds(b, DEC_SEQ, stride=SB)
        s0 = s_ref[b]
        o_scr[rows, :] = o_scr[rows, :] + _dot(qa_scr[rows, :].astype(bf16), s0.astype(bf16))
        lhs = jnp.concatenate([kb_scr[rows, :], fs_scr[rows, :]], axis=0).astype(bf16)
        v8 = v_scr[rows, :].astype(bf16)
        rhs = jnp.concatenate([jnp.concatenate([v8, zero8], axis=1), jnp.concatenate([zero8, ones3], axis=1)], axis=0)
        r = _dot_tn(lhs, rhs)
        so_ref[b] = r[:, DV:] * s0 + r[:, :DV]
        return carry

    lax.fori_loop(0, SB, per_batch, 0, unroll=HS_UNROLL)
    gain = gain_ref[...]
    for t in range(DEC_SEQ):
        on_ref[slab(t), :] = (_rms(o_scr[slab(t), :]) * gain * og_ref[slab(t), :].astype(f32)).astype(bf16)


def _hgrn_sample(zqo, zf, zi, state_hgrn, hs_prev, gain, layer):
    rb0 = T_P // TR
    spec = lambda off: pl.BlockSpec((TR, DK), lambda g, h: (rb0 + g, off + h))
    sspec = pl.BlockSpec((None, SB, None, DK, DV), lambda g, h: (layer, g, h, 0, 0))
    ins = [zqo, zf, zi, zqo, state_hgrn, gain.reshape(DEPTH, 1, D)]
    in_specs = [spec(0), spec(0), spec(0), spec(H_B), sspec,
                pl.BlockSpec((None, 1, DV), lambda g, h: (layer, 0, h))]
    aliases = {}
    if hs_prev is not None:
        ins.append(hs_prev)
        in_specs.append(pl.BlockSpec(memory_space=pl.ANY))
        aliases[6] = 1
    return pl.pallas_call(
        _hgrn_sample_kernel,
        out_shape=(jax.ShapeDtypeStruct((T_S, D), bf16), jax.ShapeDtypeStruct(state_hgrn.shape, f32)),
        grid=(N_SG, H_B),
        in_specs=in_specs,
        out_specs=(pl.BlockSpec((TR, DV), lambda g, h: (g, h)), sspec),
        scratch_shapes=[pltpu.VMEM((TR, DV), f32)] * 5,
        input_output_aliases=aliases,
        compiler_params=_cparams(("arbitrary", "arbitrary")),
        name=f"hgrn_sample{layer}",
    )(*ins)


def _mix_kernel(onp_ref, ons_ref, w_ref, a_ref, gb_ref, o_ref, wbf_ref):
    m = pl.program_id(1)

    @pl.when(m == 0)
    def _():
        wbf_ref[...] = w_ref[...].astype(bf16)

    on = jnp.where(m < N_PMB, onp_ref[...], ons_ref[...])
    o_ref[...] = (a_ref[...].astype(f32) + gb_ref[...].astype(f32) * _dot(on, wbf_ref[...])).astype(bf16)


def _branch_mix(on_p, on_s, w_bb, a_gated, zg, layer):
    return pl.pallas_call(
        _mix_kernel,
        out_shape=jax.ShapeDtypeStruct((T, D), bf16),
        grid=(D // TN, N_MB),
        in_specs=[
            pl.BlockSpec((TM, D), lambda n, m: (jnp.minimum(m, N_PMB - 1), 0)),
            pl.BlockSpec((TM, D), lambda n, m: (jnp.maximum(m - N_PMB, 0), 0), pipeline_mode=pl.Buffered(1)),
            pl.BlockSpec((None, D, TN), lambda n, m: (layer, 0, n), pipeline_mode=pl.Buffered(1)),
            pl.BlockSpec((TM, TN), lambda n, m: (m, n)),
            pl.BlockSpec((TM, TN), lambda n, m: (m, D // TN + n)),
        ],
        out_specs=pl.BlockSpec((TM, TN), lambda n, m: (m, n)),
        scratch_shapes=[pltpu.VMEM((D, TN), bf16)],
        compiler_params=_cparams(("arbitrary", "arbitrary")),
        name=f"branch_mix{layer}",
    )(on_p, on_s, w_bb, a_gated, zg)


def _outproj_kernel(y_ref, w_ref, *rest, split_x):
    *x_refs, n2_ref, g1_ref, sh_ref, sc_ref, xo_ref, h_ref = rest
    xo_ref[...] = _dot(y_ref[...], w_ref[...])
    n2 = n2_ref[...]
    g1 = g1_ref[...]
    sh = sh_ref[...]
    sc1 = 1.0 + sc_ref[...]
    is_prompt = pl.program_id(0) < N_PRB
    for s in range(TR // SB):
        rows = slice(s * SB, (s + 1) * SB)
        x = jnp.where(is_prompt, x_refs[0][rows, :], x_refs[1][rows, :]) if split_x else x_refs[0][rows, :]
        xm = x + g1 * xo_ref[rows, :]
        xo_ref[rows, :] = xm
        h_ref[rows, :] = (_rms(xm) * n2 * sc1 + sh).astype(bf16)


def _outproj(y, w_out_bf, xs, n2, mods, layer):
    split_x = len(xs) == 2
    return pl.pallas_call(
        functools.partial(_outproj_kernel, split_x=split_x),
        out_shape=(jax.ShapeDtypeStruct((T, D), f32), jax.ShapeDtypeStruct((T, D), bf16)),
        grid=(N_RB,),
        in_specs=[
            pl.BlockSpec((TR, D), lambda i: (i, 0)),
            pl.BlockSpec((None, D, D), lambda i: (layer, 0, 0), pipeline_mode=pl.Buffered(1)),
            *(_x_specs(TR) if split_x else [pl.BlockSpec((TR, D), lambda i: (i, 0))]),
            pl.BlockSpec((None, 1, D), lambda i: (layer, 0, 0)),
            _mod_spec(layer, _MOD_G1), _mod_spec(layer, _MOD_SH2), _mod_spec(layer, _MOD_SC2),
        ],
        out_specs=(pl.BlockSpec((TR, D), lambda i: (i, 0)), pl.BlockSpec((TR, D), lambda i: (i, 0))),
        compiler_params=_cparams(("arbitrary",)),
        name=f"outproj{layer}",
    )(y, w_out_bf, *xs, n2.reshape(DEPTH, 1, D), mods, mods, mods)


def _conv3(cur, m1, m2, cw_ref, cb_ref):
    return cb_ref[...] + cw_ref[0:1, :] * m2 + cw_ref[1:2, :] * m1 + cw_ref[2:3, :] * cur


def _up_kernel(h_ref, wa_ref, wb_ref, cwa_ref, cwb_ref, cba_ref, cbb_ref, sa_ref, sb_ref,
               act_ref, ta_ref, tb_ref, wabf_ref, wbbf_ref, haloa_ref, halob_ref):
    m = pl.program_id(1)

    @pl.when(m == 0)
    def _():
        wabf_ref[...] = wa_ref[...].astype(bf16)
        wbbf_ref[...] = wb_ref[...].astype(bf16)
        ta_ref[...] = jnp.zeros_like(ta_ref)
        tb_ref[...] = jnp.zeros_like(tb_ref)
        haloa_ref[...] = jnp.zeros_like(haloa_ref)
        halob_ref[...] = jnp.zeros_like(halob_ref)

    h = h_ref[...]
    acc_a = _dot(h, wabf_ref[...])
    acc_b = _dot(h, wbbf_ref[...])
    seq_start = (m % MB_PER_SEQ) == 0
    r8 = lax.broadcasted_iota(jnp.int32, (8, TC_FF), 0)

    def conv(cur, halo_ref, cw_ref, cb_ref):
        body = _conv3(cur, pltpu.roll(cur, 1, 0), pltpu.roll(cur, 2, 0), cw_ref, cb_ref)
        halo = jnp.where(seq_start, 0.0, halo_ref[...])
        cur8 = cur[0:8, :]
        m1 = jnp.where(r8 < 1, pltpu.roll(halo, 1, 0), pltpu.roll(cur8, 1, 0))
        m2 = jnp.where(r8 < 2, pltpu.roll(halo, 2, 0), pltpu.roll(cur8, 2, 0))
        return body, _conv3(cur8, m1, m2, cw_ref, cb_ref)

    ca, ca8 = conv(acc_a, haloa_ref, cwa_ref, cba_ref)
    cb, cb8 = conv(acc_b, halob_ref, cwb_ref, cbb_ref)
    act_ref[...] = (jax.nn.gelu(ca) * cb).astype(bf16)
    act_ref[0:8, :] = (jax.nn.gelu(ca8) * cb8).astype(bf16)
    ta8 = acc_a[TM - 8:TM, :]
    tb8 = acc_b[TM - 8:TM, :]
    haloa_ref[...] = ta8
    halob_ref[...] = tb8

    @pl.when((m < N_PMB) & (m % MB_PER_SEQ == MB_PER_SEQ - 1))
    def _():
        r = pl.ds(pl.multiple_of((m // MB_PER_SEQ) * 8, 8), 8)
        ta_ref[r, :] = ta8
        tb_ref[r, :] = tb8

    @pl.when(m == N_PMB)
    def _():
        def conv_s(acc, st_ref, cw_ref, cb_ref, g, t):
            slab = lambda j: acc[(g * DEC_SEQ + j) * SB:(g * DEC_SEQ + j + 1) * SB, :]
            m1 = slab(t - 1) if t >= 1 else st_ref[1, g]
            m2 = slab(t - 2) if t >= 2 else st_ref[t, g]
            return _conv3(slab(t), m1, m2, cw_ref, cb_ref)

        for g in range(N_SG):
            for t in range(DEC_SEQ):
                ca_ = conv_s(acc_a, sa_ref, cwa_ref, cba_ref, g, t)
                cb_ = conv_s(acc_b, sb_ref, cwb_ref, cbb_ref, g, t)
                r0 = (g * DEC_SEQ + t) * SB
                act_ref[r0:r0 + SB, :] = (jax.nn.gelu(ca_) * cb_).astype(bf16)
            for j in range(CONV_W - 1):
                r0 = (g * DEC_SEQ + DEC_SEQ - (CONV_W - 1) + j) * SB
                d0 = 8 * 16 + (g * (CONV_W - 1) + j) * SB
                ta_ref[d0:d0 + SB, :] = acc_a[r0:r0 + SB, :]
                tb_ref[d0:d0 + SB, :] = acc_b[r0:r0 + SB, :]


def _up_act(h, w_up, conv_w, conv_b, conv_state, layer):
    nb = N_CFF
    cws = lambda off: pl.BlockSpec((None, CONV_W, TC_FF), lambda n, m: (layer, 0, off + n))
    cbs = lambda off: pl.BlockSpec((None, 1, TC_FF), lambda n, m: (layer, 0, off + n))
    sts = lambda off: pl.BlockSpec((CONV_W - 1, N_SG, SB, TC_FF), lambda n, m: (0, 0, 0, off + n))
    ws = lambda off: pl.BlockSpec((None, D, TC_FF), lambda n, m: (layer, 0, off + n))
    cb3 = conv_b.reshape(DEPTH, 1, 2 * D_FF)
    return pl.pallas_call(
        _up_kernel,
        out_shape=(jax.ShapeDtypeStruct((T, D_FF), bf16), jax.ShapeDtypeStruct((TAIL_ROWS, D_FF), f32),
                   jax.ShapeDtypeStruct((TAIL_ROWS, D_FF), f32)),
        grid=(nb, N_MB),
        in_specs=[pl.BlockSpec((TM, D), lambda n, m: (m, 0)), ws(0), ws(nb), cws(0), cws(nb), cbs(0), cbs(nb),
                  sts(0), sts(nb)],
        out_specs=(pl.BlockSpec((TM, TC_FF), lambda n, m: (m, n)),
                   pl.BlockSpec((TAIL_ROWS, TC_FF), lambda n, m: (0, n)),
                   pl.BlockSpec((TAIL_ROWS, TC_FF), lambda n, m: (0, n))),
        scratch_shapes=[pltpu.VMEM((D, TC_FF), bf16), pltpu.VMEM((D, TC_FF), bf16),
                        pltpu.VMEM((8, TC_FF), f32), pltpu.VMEM((8, TC_FF), f32)],
        compiler_params=_cparams(("arbitrary", "arbitrary")),
        name=f"up_act{layer}",
    )(h, w_up, w_up, conv_w, conv_w, cb3, cb3, conv_state, conv_state)


def _ffn_kernel(act_ref, wd_ref, xm_ref, g2_ref, gain_ref, sh_ref, sc_ref, xo_ref, ho_ref):
    xo_ref[...] = _dot(act_ref[...], wd_ref[...])
    g2 = g2_ref[...]
    gain = gain_ref[...]
    for s in range(TR_FF // SB):
        rows = slice(s * SB, (s + 1) * SB)
        xn = xm_ref[rows, :] + g2 * xo_ref[rows, :]
        xo_ref[rows, :] = xn
        ho_ref[rows, :] = (_rms(xn) * gain * (1.0 + sc_ref[...]) + sh_ref[...]).astype(bf16)


def _ffn_final_kernel(act_ref, wd_ref, xm_ref, g2_ref, gain_ref, yp_ref, ys_ref, y_scr):
    y_scr[...] = _dot(act_ref[...], wd_ref[...])
    g2 = g2_ref[...]
    gain = gain_ref[...]
    for s in range(TR_FF // SB):
        rows = slice(s * SB, (s + 1) * SB)
        y_scr[rows, :] = _rms(xm_ref[rows, :] + g2 * y_scr[rows, :]) * gain
    is_prompt = pl.program_id(0) < T_P // TR_FF

    @pl.when(is_prompt)
    def _():
        yp_ref[...] = y_scr[...]

    @pl.when(jnp.logical_not(is_prompt))
    def _():
        ys_ref[...] = y_scr[...]


def _ffn(act, wd_bf, x_mid, mods, gain, layer, final):
    row = lambda i: (i, 0)
    n_p = T_P // TR_FF
    in_specs = [
        pl.BlockSpec((TR_FF, D_FF), row),
        pl.BlockSpec((None, D_FF, D), lambda i: (layer, 0, 0), pipeline_mode=pl.Buffered(1)),
        pl.BlockSpec((TR_FF, D), row),
        _mod_spec(layer, _MOD_G2, TR_FF),
        pl.BlockSpec((1, D), lambda i: (0, 0)),
    ]
    if final:
        return pl.pallas_call(
            _ffn_final_kernel,
            out_shape=(jax.ShapeDtypeStruct((T_P, D), f32), jax.ShapeDtypeStruct((T_S, D), f32)),
            grid=(T // TR_FF,),
            in_specs=in_specs,
            out_specs=(pl.BlockSpec((TR_FF, D), lambda i: (jnp.minimum(i, n_p - 1), 0)),
                       pl.BlockSpec((TR_FF, D), lambda i: (jnp.maximum(i - n_p, 0), 0))),
            scratch_shapes=[pltpu.VMEM((TR_FF, D), f32)],
            compiler_params=_cparams(("arbitrary",)),
            name=f"ffn{layer}",
        )(act, wd_bf, x_mid, mods, gain.reshape(1, D))
    return pl.pallas_call(
        _ffn_kernel,
        out_shape=(jax.ShapeDtypeStruct((T, D), f32), jax.ShapeDtypeStruct((T, D), bf16)),
        grid=(T // TR_FF,),
        in_specs=in_specs + [_mod_spec(layer + 1, _MOD_SH1, TR_FF), _mod_spec(layer + 1, _MOD_SC1, TR_FF)],
        out_specs=(pl.BlockSpec((TR_FF, D), row), pl.BlockSpec((TR_FF, D), row)),
        compiler_params=_cparams(("arbitrary",)),
        name=f"ffn{layer}",
    )(act, wd_bf, x_mid, mods, gain.reshape(1, D), mods, mods)


def _to_rows(a):
    f = a.shape[-1]
    return a.reshape(N_SG, SB, DEC_SEQ, f).transpose(0, 2, 1, 3).reshape(T_S, f)


def _from_rows(a, steps=DEC_SEQ):
    f = a.shape[-1]
    return a.reshape(N_SG, steps, SB, f).transpose(0, 2, 1, 3).reshape(DEC_BATCH, steps, f)


def kernel(x_prompt, x_sample, c_prompt, c_sample, state_hgrn, state_ffn_conv, ada_w, ada_b, norm1_g, norm2_g,
           w_in, gmlp_v_g, gmlp_ws, gmlp_bs, w_branch_a, hgrn_lower_bounds, hgrn_norm_g, w_branch_b, w_out,
           w_up, conv_w, conv_b, w_down, final_norm_g):
    xs = (x_prompt.reshape(T_P, D), _to_rows(x_sample))
    c_all = jnp.concatenate([c_sample, c_prompt, jnp.zeros((4, D), f32)], axis=0)
    mods = _ada(c_all, ada_w, ada_b)

    wout_bf = w_out.astype(bf16)
    wd_bf = w_down.astype(bf16)
    conv_st = state_ffn_conv.transpose(0, 2, 1, 3).reshape(DEPTH, CONV_W - 1, N_SG, SB, 2 * D_FF)

    h = _norm_mod(*xs, norm1_g[0], mods, 0)
    hs_all = None
    hp, cp, cs, vp, vs = [], [], [], [], []
    y_p = y_s = None
    for l in range(DEPTH):
        zu, zv, zqo, zf, zi, zg = _inproj(h, w_in, hgrn_lower_bounds, gmlp_v_g, l)
        a_gated = _gmlp(zu, zv, zg, gmlp_ws[l], gmlp_bs[l], w_branch_a, l)
        on_p, s_p = _hgrn_prompt(zqo, zf, zi, hgrn_norm_g, l)
        on_s, hs_all = _hgrn_sample(zqo, zf, zi, state_hgrn, hs_all, hgrn_norm_g, l)
        ymix = _branch_mix(on_p, on_s, w_branch_b, a_gated, zg, l)
        x_mid, h2 = _outproj(ymix, wout_bf, xs, norm2_g, mods, l)
        act, tail_a, tail_b = _up_act(h2, w_up, conv_w, conv_b, conv_st[l], l)
        if l == DEPTH - 1:
            y_p, y_s = _ffn(act, wd_bf, x_mid, mods, final_norm_g, l, True)
        else:
            x, h = _ffn(act, wd_bf, x_mid, mods, norm1_g[l + 1], l, False)
            xs = (x,)
        tail = jnp.concatenate([tail_a, tail_b], axis=1)
        hp.append(s_p)
        cp.append(tail[:8 * BATCH].reshape(BATCH, 8, 2 * D_FF)[:, 8 - (CONV_W - 1):])
        cs.append(_from_rows(tail[8 * 16:], CONV_W - 1))
        vp.append(jnp.stack([zv[(b + 1) * SEQ - CHUNK:(b + 1) * SEQ] for b in range(BATCH)]))
        vs.append(_from_rows(zv[T_P:]))

    y_prompt = y_p.reshape(BATCH, SEQ, D)
    y_sample = _from_rows(y_s)
    return (y_prompt, y_sample, jnp.stack(hp), hs_all, jnp.stack(cp), jnp.stack(cs), jnp.stack(vp), jnp.stack(vs))
```

```python
import functools

import numpy as np
import jax
import jax.numpy as jnp
from jax import lax
from jax.experimental import pallas as pl
from jax.experimental.pallas import tpu as pltpu

f32 = jnp.float32
bf16 = jnp.bfloat16

D = 2048
BATCH = 4
SEQ = 2048
DEPTH = 2
DEC_BATCH = 128
DEC_SEQ = 8
G_A = 8
DG_A = 128
D_A = G_A * DG_A
CHUNK = 128
H_B = 16
DK = 128
DV = 128
D_FF = 5632
CONV_W = 3
N_IN = 2 * D_A + 6 * D
EPS = 1e-6

T_P = BATCH * SEQ
T_S = DEC_BATCH * DEC_SEQ
T = T_P + T_S
SB = 64
N_SG = DEC_BATCH // SB
TR = DEC_SEQ * SB
N_RB = T // TR
N_PRB = T_P // TR
RB_PER_SEQ = SEQ // TR
TM = 1024
N_MB = T // TM
N_PMB = T_P // TM
MB_PER_SEQ = SEQ // TM
TN = 1024
TC_FF = 512
N_CFF = D_FF // TC_FF
TR_FF = 256
HG_C = 128
HG_NH = 8
HG_BLOCK_LEVELS = (64, 32, 16)
HG_MASK_LEVELS = (8, 4, 2, 1)
HS_UNROLL = 32
SUBLANES = 8
LOG2E = 1.4426950408889634
TAIL_P = 128
TAIL_ROWS = TAIL_P + (CONV_W - 1) * DEC_BATCH
assert BATCH * SUBLANES <= TAIL_P

_U0, _V0, _Q0, _F0, _I0, _OG0, _GA0 = 0, 1, 2, 4, 6, 8, 10
_MOD_SH1, _MOD_SC1, _MOD_G1, _MOD_SH2, _MOD_SC2, _MOD_G2 = range(6)
N_MODJ = BATCH + N_SG


def _cparams(sem):
    return pltpu.CompilerParams(dimension_semantics=sem)


def _hgrn_level_table(c):
    t = np.arange(c)
    lev = np.full((c, c), -1, np.int32)
    lev[t, t] = 0
    for h in HG_MASK_LEVELS:
        upper = (t % (2 * h)) >= h
        same = (t[:, None] // (2 * h)) == (t[None, :] // (2 * h))
        lev[same & upper[:, None] & ~upper[None, :]] = h
    return lev


_LEV_NP = _hgrn_level_table(HG_C)


def _rms(x, eps=EPS):
    return x * lax.rsqrt(jnp.mean(x * x, axis=-1, keepdims=True) + eps)


def _dot(a, b):
    return jnp.dot(a, b, preferred_element_type=f32)


def _dot_nt(a, b):
    return lax.dot_general(a, b, (((1,), (1,)), ((), ())), preferred_element_type=f32)


def _dot_tn(a, b):
    return lax.dot_general(a, b, (((0,), (0,)), ((), ())), preferred_element_type=f32)


def _modj(i, rows):
    return jnp.where(i < T_P // rows, i // (SEQ // rows), BATCH + (i - T_P // rows) // (TR // rows))


def _mod_spec(layer, kind, rows=TR):
    return pl.BlockSpec((None, None, None, SB, D), lambda i, *_: (layer, kind, _modj(i, rows), 0, 0))


def _ada_kernel(c_ref, w_ref, b_ref, o_ref):
    c = c_ref[...]
    a = (c * jax.nn.sigmoid(c)).astype(bf16)
    r = _dot(a, w_ref[...].astype(bf16)) + b_ref[...]
    for g in range(N_SG):
        o_ref[BATCH + g] = r[g * SB:(g + 1) * SB]
    for j in range(BATCH):
        o_ref[j] = jnp.broadcast_to(r[DEC_BATCH + j:DEC_BATCH + j + 1], (SB, TN))


def _ada(c_all, ada_w, ada_b):
    nrow = c_all.shape[0]
    ncol = 6 * D // TN
    per_kind = D // TN
    return pl.pallas_call(
        _ada_kernel,
        out_shape=jax.ShapeDtypeStruct((DEPTH, 6, N_MODJ, SB, D), f32),
        grid=(DEPTH, ncol),
        in_specs=[
            pl.BlockSpec((nrow, D), lambda l, n: (0, 0)),
            pl.BlockSpec((None, D, TN), lambda l, n: (l, 0, n)),
            pl.BlockSpec((None, 1, TN), lambda l, n: (l, 0, n)),
        ],
        out_specs=pl.BlockSpec((None, None, N_MODJ, SB, TN), lambda l, n: (l, n // per_kind, 0, 0, n % per_kind)),
        compiler_params=_cparams(("arbitrary", "arbitrary")),
        name="ada_mod",
    )(c_all, ada_w, ada_b.reshape(DEPTH, 1, 6 * D))


def _x_specs(rows):
    n_p = T_P // rows
    return [pl.BlockSpec((rows, D), lambda i: (jnp.minimum(i, n_p - 1), 0)),
            pl.BlockSpec((rows, D), lambda i: (jnp.maximum(i - n_p, 0), 0))]


def _norm_mod_kernel(xp_ref, xs_ref, g_ref, sh_ref, sc_ref, o_ref):
    g = g_ref[...]
    sh = sh_ref[...]
    sc1 = 1.0 + sc_ref[...]
    is_prompt = pl.program_id(0) < N_PRB
    for s in range(TR // SB):
        rows = slice(s * SB, (s + 1) * SB)
        x = jnp.where(is_prompt, xp_ref[rows, :], xs_ref[rows, :])
        o_ref[rows, :] = (_rms(x) * g * sc1 + sh).astype(bf16)


def _norm_mod(xp, xs, gain, mods, layer):
    return pl.pallas_call(
        _norm_mod_kernel,
        out_shape=jax.ShapeDtypeStruct((T, D), bf16),
        grid=(N_RB,),
        in_specs=[
            *_x_specs(TR),
            pl.BlockSpec((1, D), lambda i: (0, 0)),
            _mod_spec(layer, _MOD_SH1),
            _mod_spec(layer, _MOD_SC1),
        ],
        out_specs=pl.BlockSpec((TR, D), lambda i: (i, 0)),
        compiler_params=_cparams(("arbitrary",)),
        name="norm_mod",
    )(xp, xs, gain.reshape(1, D), mods, mods)


def _proj_kernel(h_ref, w_ref, *rest, epilogue):
    *extra, o_ref, wbf_ref = rest

    @pl.when(pl.program_id(1) == 0)
    def _():
        wbf_ref[...] = w_ref[...].astype(bf16)

    epilogue(_dot(h_ref[...], wbf_ref[...]), o_ref, *extra)


def _epi_gelu(acc, o_ref):
    o_ref[...] = jax.nn.gelu(acc).astype(o_ref.dtype)


def _epi_v(acc, o_ref, vg_ref):
    v = jax.nn.gelu(acc)
    for g in range(G_A):
        cols = slice(g * DG_A, (g + 1) * DG_A)
        o_ref[:, cols] = _rms(v[:, cols]) * vg_ref[:, cols]


def _epi_silu(acc, o_ref, scale_ref):
    o_ref[...] = (acc * jax.nn.sigmoid(acc) * scale_ref[pl.program_id(0)]).astype(o_ref.dtype)


def _epi_logf(acc, o_ref, lb_ref, *, layer):
    p = jax.nn.softmax(lb_ref[...], axis=0)
    lb = jnp.zeros((1, TN), f32)
    for j in range(1, layer + 1):
        lb = lb + p[j:j + 1]
    a = jnp.log(lb)
    b = jnp.log1p(-lb) + jnp.minimum(acc, 0.0) - jnp.log(1.0 + jnp.exp(-jnp.abs(acc)))
    o_ref[...] = jnp.maximum(a, b) + jnp.log(1.0 + jnp.exp(-jnp.abs(a - b)))


def _epi_id(acc, o_ref):
    o_ref[...] = acc.astype(o_ref.dtype)


def _epi_sigmoid(acc, o_ref):
    o_ref[...] = jax.nn.sigmoid(acc).astype(o_ref.dtype)


def _proj(name, h, w_in, layer, col_of, n_tiles, out_dtype, epilogue, extra=(), extra_specs=()):
    return pl.pallas_call(
        functools.partial(_proj_kernel, epilogue=epilogue),
        out_shape=jax.ShapeDtypeStruct((T, n_tiles * TN), out_dtype),
        grid=(n_tiles, N_MB),
        in_specs=[
            pl.BlockSpec((TM, D), lambda n, m: (m, 0)),
            pl.BlockSpec((None, D, TN), lambda n, m: (layer, 0, col_of(n))),
            *extra_specs,
        ],
        out_specs=pl.BlockSpec((TM, TN), lambda n, m: (m, n)),
        scratch_shapes=[pltpu.VMEM((D, TN), bf16)],
        compiler_params=_cparams(("arbitrary", "arbitrary")),
        name=f"{name}{layer}",
    )(h, w_in, *extra)


def _inproj(h, w_in, lower_bounds, v_gain, layer):
    smem = pl.BlockSpec(memory_space=pltpu.SMEM)
    nd = D // TN
    zu = _proj("in_u", h, w_in, layer, lambda n: _U0 + n, D_A // TN, bf16, _epi_gelu)
    zv = _proj("in_v", h, w_in, layer, lambda n: _V0 + n, D_A // TN, f32, _epi_v,
               (v_gain.reshape(DEPTH, 1, D_A),), (pl.BlockSpec((None, 1, D_A), lambda n, m: (layer, 0, 0)),))
    scales = jnp.asarray([DK ** -0.5] * nd + [1.0] * nd, f32)
    zqo = _proj("in_qo", h, w_in, layer, lambda n: jnp.where(n < nd, _Q0 + n, _OG0 + n - nd), 2 * nd, bf16, _epi_silu,
                (scales,), (smem,))
    zf = _proj("in_f", h, w_in, layer, lambda n: _F0 + n, nd, f32, functools.partial(_epi_logf, layer=layer),
               (lower_bounds,), (pl.BlockSpec((DEPTH, TN), lambda n, m: (0, n)),))
    zi = _proj("in_i", h, w_in, layer, lambda n: _I0 + n, nd, bf16, _epi_id)
    zg = _proj("in_g", h, w_in, layer, lambda n: _GA0 + n, 2 * nd, bf16, _epi_sigmoid)
    return zu, zv, zqo, zf, zi, zg


def _gmlp_kernel(u_ref, v_ref, ga_ref, ws_ref, bias_ref, w8_ref, b8_ref, wba_ref, o_ref, wm_ref, ya_ref, wbf_ref):
    i = pl.program_id(0)

    @pl.when(i == 0)
    def _():
        wbf_ref[...] = wba_ref[...].astype(bf16)
        r = lax.broadcasted_iota(jnp.int32, (CHUNK, CHUNK), 0)
        c = lax.broadcasted_iota(jnp.int32, (CHUNK, CHUNK), 1)
        for g in range(G_A):
            wm_ref[g] = jnp.where(r >= c, ws_ref[g], 0.0).astype(bf16)

    @pl.when(i < N_PRB)
    def _():
        for ch in range(TR // CHUNK):
            rows = slice(ch * CHUNK, (ch + 1) * CHUNK)
            for g in range(G_A):
                cols = slice(g * DG_A, (g + 1) * DG_A)
                s = _dot(wm_ref[g], v_ref[rows, cols].astype(bf16)) + bias_ref[:, cols]
                ya_ref[rows, cols] = (u_ref[rows, cols].astype(f32) * s).astype(bf16)

    @pl.when(i >= N_PRB)
    def _():
        for g in range(G_A):
            cols = slice(g * DG_A, (g + 1) * DG_A)
            vs = [v_ref[s * SB:(s + 1) * SB, cols] for s in range(DEC_SEQ)]
            for t in range(DEC_SEQ):
                w0 = (g * DEC_SEQ + t) * DEC_SEQ
                acc = vs[0] * w8_ref[w0]
                for s in range(1, t + 1):
                    acc = acc + vs[s] * w8_ref[w0 + s]
                acc = acc + b8_ref[g * DEC_SEQ + t]
                ya_ref[t * SB:(t + 1) * SB, cols] = (u_ref[t * SB:(t + 1) * SB, cols].astype(f32) * acc).astype(bf16)

    o_ref[...] = (ga_ref[...].astype(f32) * _dot(ya_ref[...], wbf_ref[...])).astype(bf16)


def _gmlp(zu, zv, zg, ws, bs, w_ba, layer):
    bias_full = jnp.repeat(bs.T, DG_A, axis=1)
    w8 = ws[:, :DEC_SEQ, :DEC_SEQ].reshape(G_A * DEC_SEQ * DEC_SEQ)
    b8 = bs[:, :DEC_SEQ].reshape(G_A * DEC_SEQ)
    return pl.pallas_call(
        _gmlp_kernel,
        out_shape=jax.ShapeDtypeStruct((T, D), bf16),
        grid=(N_RB,),
        in_specs=[
            pl.BlockSpec((TR, D_A), lambda i: (i, 0)),
            pl.BlockSpec((TR, D_A), lambda i: (i, 0)),
            pl.BlockSpec((TR, D), lambda i: (i, 0)),
            pl.BlockSpec((G_A, CHUNK, CHUNK), lambda i: (0, 0, 0)),
            pl.BlockSpec((CHUNK, D_A), lambda i: (0, 0)),
            pl.BlockSpec(memory_space=pltpu.SMEM),
            pl.BlockSpec(memory_space=pltpu.SMEM),
            pl.BlockSpec((None, D_A, D), lambda i: (layer, 0, 0), pipeline_mode=pl.Buffered(1)),
        ],
        out_specs=pl.BlockSpec((TR, D), lambda i: (i, 0)),
        scratch_shapes=[pltpu.VMEM((G_A, CHUNK, CHUNK), bf16), pltpu.VMEM((TR, D_A), bf16),
                        pltpu.VMEM((D_A, D), bf16)],
        compiler_params=_cparams(("arbitrary",)),
        name=f"gmlp{layer}",
    )(zu, zv, zg, ws, bias_full, w8, b8, w_ba)


def _bcast_row(x8, r):
    return jnp.broadcast_to(x8[r:r + 1, :], x8.shape)


def _cat_rows(tiles):
    return tiles[0] if len(tiles) == 1 else jnp.concatenate(tiles, axis=0)


def _hgrn_chunk(q, g, v, sts, lev, r8, nh):
    c = HG_C
    nv = c // 8
    w = nh * DK
    hs = [slice(i * DK, (i + 1) * DK) for i in range(nh)]
    gl = g * LOG2E
    x_t = []
    for j in range(nv):
        x = gl[8 * j:8 * j + 8, :]
        for sh in (1, 2, 4):
            x = x + jnp.where(r8 >= sh, pltpu.roll(x, sh, 0), 0.0)
        x_t.append(x)
    last = [_bcast_row(x, 7) for x in x_t]
    sh = 1
    while sh < nv:
        last = [last[j] + last[j - sh] if j >= sh else last[j] for j in range(nv)]
        sh *= 2
    a_t = [x_t[0]] + [x_t[j] + last[j - 1] for j in range(1, nv)]
    f = jnp.exp2(gl)
    kk = 1.0 - f
    q_t = [q[8 * j:8 * j + 8, :] for j in range(nv)]
    k_t = [kk[8 * j:8 * j + 8, :] for j in range(nv)]
    a = _cat_rows(a_t)
    odd = (lax.broadcasted_iota(jnp.int32, (c, w), 0) & 1) == 1

    qb = q.astype(bf16)
    kb0 = kk.astype(bf16)
    sc = [jnp.where(lev == 0, _dot_nt(qb[:, hh], kb0[:, hh]), 0.0) for hh in hs]
    for h in HG_MASK_LEVELS:
        if h == 8:
            e = _cat_rows([(a_t[j] - last[j - 1]) if j % 2 else (last[j] - a_t[j]) for j in range(nv)])
        elif h == 4:
            d = _cat_rows([t - _bcast_row(t, 3) for t in a_t])
            e = jnp.minimum(d, -d)
        elif h == 2:
            d = _cat_rows([t - jnp.where(r8 < 4, _bcast_row(t, 1), _bcast_row(t, 5)) for t in a_t])
            e = jnp.minimum(d, -d)
        if h >= 2:
            e = jnp.exp2(e)
        else:
            e = jnp.where(odd, f, 1.0)
        qe = (q * e).astype(bf16)
        ke = (kk * e).astype(bf16)
        sc = [jnp.where(lev == h, _dot_nt(qe[:, hh], ke[:, hh]), sc[i]) for i, hh in enumerate(hs)]
    sc_t = [[s_[8 * j:8 * j + 8, :] for j in range(nv)] for s_ in sc]
    zero8 = jnp.zeros((8, w), f32)
    for h in HG_BLOCK_LEVELS:
        ht = h // 8
        for blk in range(c // (2 * h)):
            lo = range(blk * 2 * ht, blk * 2 * ht + ht)
            up = range(blk * 2 * ht + ht, (blk + 1) * 2 * ht)
            ref = last[blk * 2 * ht + ht - 1]
            qt = _cat_rows([q_t[j] * jnp.exp2(a_t[j] - ref) for j in up]).astype(bf16)
            kz = _cat_rows([k_t[j] * jnp.exp2(ref - a_t[j]) if j in lo else zero8 for j in range(nv)]).astype(bf16)
            for i, hh in enumerate(hs):
                s_ = _dot_nt(qt[:, hh], kz[:, hh])
                for r, j in enumerate(up):
                    sc_t[i][j] = sc_t[i][j] + s_[8 * r:8 * r + 8, :]
    qa = (q * jnp.exp2(a)).astype(bf16)
    a_last = last[-1]
    kb = _cat_rows([k_t[j] * jnp.exp2(a_last - a_t[j]) for j in range(nv)]).astype(bf16)
    fl = jnp.exp2(a_last[0:1, :])
    o, st_new = [], []
    for i, hh in enumerate(hs):
        o.append(_dot_nt(qa[:, hh], sts[i].astype(bf16)) + _dot(_cat_rows(sc_t[i]).astype(bf16), v[:, hh]))
        st_new.append(fl[:, hh] * sts[i] + _dot_tn(v[:, hh], kb[:, hh]))
    return o, st_new


def _hgrn_prompt_kernel(q_ref, g_ref, v_ref, og_ref, lev_ref, gain_ref, on_ref, so_ref, st_ref):
    r8 = lax.broadcasted_iota(jnp.int32, (8, HG_NH * DK), 0)
    for hh in range(HG_NH):
        st_ref[hh] = jnp.zeros((DV, DK), f32)

    def chunk(c, carry):
        r = pl.ds(pl.multiple_of(c * HG_C, HG_C), HG_C)
        o, st_new = _hgrn_chunk(q_ref[r, :].astype(f32), g_ref[r, :], v_ref[r, :],
                                [st_ref[hh] for hh in range(HG_NH)], lev_ref[...], r8, HG_NH)
        for hh in range(HG_NH):
            cols = slice(hh * DV, (hh + 1) * DV)
            on_ref[r, cols] = (_rms(o[hh]) * gain_ref[:, cols] * og_ref[r, cols].astype(f32)).astype(bf16)
            st_ref[hh] = st_new[hh]
        return carry

    lax.fori_loop(0, SEQ // HG_C, chunk, 0)
    for hh in range(HG_NH):
        so_ref[hh] = st_ref[hh].T


def _hgrn_prompt(zqo, zf, zi, gain, layer):
    ng = H_B // HG_NH
    spec = lambda off: pl.BlockSpec((SEQ, HG_NH * DK), lambda b, h: (b, off + h))
    return pl.pallas_call(
        _hgrn_prompt_kernel,
        out_shape=(jax.ShapeDtypeStruct((T_P, D), bf16), jax.ShapeDtypeStruct((BATCH, H_B, DK, DV), f32)),
        grid=(BATCH, ng),
        in_specs=[
            spec(0), spec(0), spec(0), spec(ng),
            pl.BlockSpec((HG_C, HG_C), lambda b, h: (0, 0)),
            pl.BlockSpec((None, 1, HG_NH * DV), lambda b, h: (layer, 0, h)),
        ],
        out_specs=(pl.BlockSpec((SEQ, HG_NH * DV), lambda b, h: (b, h)),
                   pl.BlockSpec((None, HG_NH, DK, DV), lambda b, h: (b, h, 0, 0))),
        scratch_shapes=[pltpu.VMEM((HG_NH, DV, DK), f32)],
        compiler_params=_cparams(("arbitrary", "arbitrary")),
        name=f"hgrn_prompt{layer}",
    )(zqo, zf, zi, zqo, jnp.asarray(_LEV_NP), gain.reshape(DEPTH, 1, D))


def _hgrn_sample_kernel(*refs):
    q_ref, g_ref, v_ref, og_ref, s_ref, gain_ref = refs[:6]
    on_ref, so_ref, qa_scr, kb_scr, v_scr, fs_scr, o_scr = refs[-7:]
    slab = lambda t: slice(t * SB, (t + 1) * SB)
    a, q, kk, v = [], [], [], []
    for t in range(DEC_SEQ):
        gl = g_ref[slab(t), :] * LOG2E
        a.append(gl if t == 0 else a[t - 1] + gl)
        kk.append(1.0 - jnp.exp2(gl))
        q.append(q_ref[slab(t), :].astype(f32))
        v.append(v_ref[slab(t), :].astype(f32))
    a_last = a[DEC_SEQ - 1]
    for t in range(DEC_SEQ):
        qa_scr[slab(t), :] = q[t] * jnp.exp2(a[t])
        kb_scr[slab(t), :] = kk[t] * jnp.exp2(a_last - a[t])
        v_scr[slab(t), :] = v[t]
        o = jnp.sum(q[t] * kk[t], axis=-1, keepdims=True) * v[t]
        for s in range(t):
            o = o + jnp.sum(q[t] * kk[s] * jnp.exp2(a[t] - a[s]), axis=-1, keepdims=True) * v[s]
        o_scr[slab(t), :] = o
    fl = jnp.exp2(a_last)
    f1 = fl.astype(bf16).astype(f32)
    r1 = fl - f1
    f2 = r1.astype(bf16).astype(f32)
    fs_scr[slab(0), :] = f1
    fs_scr[slab(1), :] = f2
    fs_scr[slab(2), :] = r1 - f2
    fs_scr[3 * SB:, :] = jnp.zeros((TR - 3 * SB, DK), f32)

    ones3 = (lax.broadcasted_iota(jnp.int32, (DEC_SEQ, DV), 0) < 3).astype(bf16)
    zero8 = jnp.zeros((DEC_SEQ, DV), bf16)

    def per_batch(b, carry):
        rows = pl.ds(b, DEC_SEQ, stride=SB)
        s0 = s_ref[b]
        o_scr[rows, :] = o_scr[rows, :] + _dot(qa_scr[rows, :].astype(bf16), s0.astype(bf16))
        lhs = jnp.concatenate([kb_scr[rows, :], fs_scr[rows, :]], axis=0).astype(bf16)
        v8 = v_scr[rows, :].astype(bf16)
        rhs = jnp.concatenate([jnp.concatenate([v8, zero8], axis=1), jnp.concatenate([zero8, ones3], axis=1)], axis=0)
        r = _dot_tn(lhs, rhs)
        so_ref[b] = r[:, DV:] * s0 + r[:, :DV]
        return carry

    lax.fori_loop(0, SB, per_batch, 0, unroll=HS_UNROLL)
    gain = gain_ref[...]
    for t in range(DEC_SEQ):
        on_ref[slab(t), :] = (_rms(o_scr[slab(t), :]) * gain * og_ref[slab(t), :].astype(f32)).astype(bf16)


def _hgrn_sample(zqo, zf, zi, state_hgrn, hs_prev, gain, layer):
    rb0 = T_P // TR
    spec = lambda off: pl.BlockSpec((TR, DK), lambda g, h: (rb0 + g, off + h))
    sspec = pl.BlockSpec((None, SB, None, DK, DV), lambda g, h: (layer, g, h, 0, 0))
    ins = [zqo, zf, zi, zqo, state_hgrn, gain.reshape(DEPTH, 1, D)]
    in_specs = [spec(0), spec(0), spec(0), spec(H_B), sspec,
                pl.BlockSpec((None, 1, DV), lambda g, h: (layer, 0, h))]
    aliases = {}
    if hs_prev is not None:
        ins.append(hs_prev)
        in_specs.append(pl.BlockSpec(memory_space=pl.ANY))
        aliases[6] = 1
    return pl.pallas_call(
        _hgrn_sample_kernel,
        out_shape=(jax.ShapeDtypeStruct((T_S, D), bf16), jax.ShapeDtypeStruct(state_hgrn.shape, f32)),
        grid=(N_SG, H_B),
        in_specs=in_specs,
        out_specs=(pl.BlockSpec((TR, DV), lambda g, h: (g, h)), sspec),
        scratch_shapes=[pltpu.VMEM((TR, DV), f32)] * 5,
        input_output_aliases=aliases,
        compiler_params=_cparams(("arbitrary", "arbitrary")),
        name=f"hgrn_sample{layer}",
    )(*ins)


def _mix_kernel(onp_ref, ons_ref, w_ref, a_ref, gb_ref, o_ref, wbf_ref):
    m = pl.program_id(1)

    @pl.when(m == 0)
    def _():
        wbf_ref[...] = w_ref[...].astype(bf16)

    on = jnp.where(m < N_PMB, onp_ref[...], ons_ref[...])
    o_ref[...] = (a_ref[...].astype(f32) + gb_ref[...].astype(f32) * _dot(on, wbf_ref[...])).astype(bf16)


def _branch_mix(on_p, on_s, w_bb, a_gated, zg, layer):
    return pl.pallas_call(
        _mix_kernel,
        out_shape=jax.ShapeDtypeStruct((T, D), bf16),
        grid=(D // TN, N_MB),
        in_specs=[
            pl.BlockSpec((TM, D), lambda n, m: (jnp.minimum(m, N_PMB - 1), 0)),
            pl.BlockSpec((TM, D), lambda n, m: (jnp.maximum(m - N_PMB, 0), 0), pipeline_mode=pl.Buffered(1)),
            pl.BlockSpec((None, D, TN), lambda n, m: (layer, 0, n)),
            pl.BlockSpec((TM, TN), lambda n, m: (m, n)),
            pl.BlockSpec((TM, TN), lambda n, m: (m, D // TN + n)),
        ],
        out_specs=pl.BlockSpec((TM, TN), lambda n, m: (m, n)),
        scratch_shapes=[pltpu.VMEM((D, TN), bf16)],
        compiler_params=_cparams(("arbitrary", "arbitrary")),
        name=f"branch_mix{layer}",
    )(on_p, on_s, w_bb, a_gated, zg)


def _outproj_kernel(y_ref, w_ref, *rest, split_x):
    *x_refs, n2_ref, g1_ref, sh_ref, sc_ref, xo_ref, h_ref = rest
    xo_ref[...] = _dot(y_ref[...], w_ref[...])
    n2 = n2_ref[...]
    g1 = g1_ref[...]
    sh = sh_ref[...]
    sc1 = 1.0 + sc_ref[...]
    is_prompt = pl.program_id(0) < N_PRB
    for s in range(TR // SB):
        rows = slice(s * SB, (s + 1) * SB)
        x = jnp.where(is_prompt, x_refs[0][rows, :], x_refs[1][rows, :]) if split_x else x_refs[0][rows, :]
        xm = x + g1 * xo_ref[rows, :]
        xo_ref[rows, :] = xm
        h_ref[rows, :] = (_rms(xm) * n2 * sc1 + sh).astype(bf16)


def _outproj(y, w_out_bf, xs, n2, mods, layer):
    split_x = len(xs) == 2
    return pl.pallas_call(
        functools.partial(_outproj_kernel, split_x=split_x),
        out_shape=(jax.ShapeDtypeStruct((T, D), f32), jax.ShapeDtypeStruct((T, D), bf16)),
        grid=(N_RB,),
        in_specs=[
            pl.BlockSpec((TR, D), lambda i: (i, 0)),
            pl.BlockSpec((None, D, D), lambda i: (layer, 0, 0), pipeline_mode=pl.Buffered(1)),
            *(_x_specs(TR) if split_x else [pl.BlockSpec((TR, D), lambda i: (i, 0))]),
            pl.BlockSpec((None, 1, D), lambda i: (layer, 0, 0)),
            _mod_spec(layer, _MOD_G1), _mod_spec(layer, _MOD_SH2), _mod_spec(layer, _MOD_SC2),
        ],
        out_specs=(pl.BlockSpec((TR, D), lambda i: (i, 0)), pl.BlockSpec((TR, D), lambda i: (i, 0))),
        compiler_params=_cparams(("arbitrary",)),
        name=f"outproj{layer}",
    )(y, w_out_bf, *xs, n2.reshape(DEPTH, 1, D), mods, mods, mods)


def _conv3(cur, m1, m2, cw_ref, cb_ref):
    return cb_ref[...] + cw_ref[0:1, :] * m2 + cw_ref[1:2, :] * m1 + cw_ref[2:3, :] * cur


def _up_kernel(h_ref, wa_ref, wb_ref, cwa_ref, cwb_ref, cba_ref, cbb_ref, sa_ref, sb_ref,
               act_ref, ta_ref, tb_ref, wabf_ref, wbbf_ref, haloa_ref, halob_ref):
    m = pl.program_id(1)

    @pl.when(m == 0)
    def _():
        wabf_ref[...] = wa_ref[...].astype(bf16)
        wbbf_ref[...] = wb_ref[...].astype(bf16)
        ta_ref[...] = jnp.zeros_like(ta_ref)
        tb_ref[...] = jnp.zeros_like(tb_ref)
        haloa_ref[...] = jnp.zeros_like(haloa_ref)
        halob_ref[...] = jnp.zeros_like(halob_ref)

    h = h_ref[...]
    acc_a = _dot(h, wabf_ref[...])
    acc_b = _dot(h, wbbf_ref[...])
    seq_start = (m % MB_PER_SEQ) == 0
    r8 = lax.broadcasted_iota(jnp.int32, (8, TC_FF), 0)

    def conv(cur, halo_ref, cw_ref, cb_ref):
        body = _conv3(cur, pltpu.roll(cur, 1, 0), pltpu.roll(cur, 2, 0), cw_ref, cb_ref)
        halo = jnp.where(seq_start, 0.0, halo_ref[...])
        cur8 = cur[0:8, :]
        m1 = jnp.where(r8 < 1, pltpu.roll(halo, 1, 0), pltpu.roll(cur8, 1, 0))
        m2 = jnp.where(r8 < 2, pltpu.roll(halo, 2, 0), pltpu.roll(cur8, 2, 0))
        return body, _conv3(cur8, m1, m2, cw_ref, cb_ref)

    ca, ca8 = conv(acc_a, haloa_ref, cwa_ref, cba_ref)
    cb, cb8 = conv(acc_b, halob_ref, cwb_ref, cbb_ref)
    act_ref[...] = (jax.nn.gelu(ca) * cb).astype(bf16)
    act_ref[0:8, :] = (jax.nn.gelu(ca8) * cb8).astype(bf16)
    ta8 = acc_a[TM - 8:TM, :]
    tb8 = acc_b[TM - 8:TM, :]
    haloa_ref[...] = ta8
    halob_ref[...] = tb8

    @pl.when((m < N_PMB) & (m % MB_PER_SEQ == MB_PER_SEQ - 1))
    def _():
        r = pl.ds(pl.multiple_of((m // MB_PER_SEQ) * SUBLANES, SUBLANES), SUBLANES)
        ta_ref[r, :] = ta8
        tb_ref[r, :] = tb8

    @pl.when(m == N_PMB)
    def _():
        def conv_s(acc, st_ref, cw_ref, cb_ref, g, t):
            slab = lambda j: acc[(g * DEC_SEQ + j) * SB:(g * DEC_SEQ + j + 1) * SB, :]
            m1 = slab(t - 1) if t >= 1 else st_ref[1, g]
            m2 = slab(t - 2) if t >= 2 else st_ref[t, g]
            return _conv3(slab(t), m1, m2, cw_ref, cb_ref)

        for g in range(N_SG):
            for t in range(DEC_SEQ):
                ca_ = conv_s(acc_a, sa_ref, cwa_ref, cba_ref, g, t)
                cb_ = conv_s(acc_b, sb_ref, cwb_ref, cbb_ref, g, t)
                r0 = (g * DEC_SEQ + t) * SB
                act_ref[r0:r0 + SB, :] = (jax.nn.gelu(ca_) * cb_).astype(bf16)
            for j in range(CONV_W - 1):
                r0 = (g * DEC_SEQ + DEC_SEQ - (CONV_W - 1) + j) * SB
                d0 = TAIL_P + (g * (CONV_W - 1) + j) * SB
                ta_ref[d0:d0 + SB, :] = acc_a[r0:r0 + SB, :]
                tb_ref[d0:d0 + SB, :] = acc_b[r0:r0 + SB, :]


def _up_act(h, w_up, conv_w, conv_b, conv_state, layer):
    nb = N_CFF
    cws = lambda off: pl.BlockSpec((None, CONV_W, TC_FF), lambda n, m: (layer, 0, off + n))
    cbs = lambda off: pl.BlockSpec((None, 1, TC_FF), lambda n, m: (layer, 0, off + n))
    sts = lambda off: pl.BlockSpec((CONV_W - 1, N_SG, SB, TC_FF), lambda n, m: (0, 0, 0, off + n))
    ws = lambda off: pl.BlockSpec((None, D, TC_FF), lambda n, m: (layer, 0, off + n))
    cb3 = conv_b.reshape(DEPTH, 1, 2 * D_FF)
    return pl.pallas_call(
        _up_kernel,
        out_shape=(jax.ShapeDtypeStruct((T, D_FF), bf16), jax.ShapeDtypeStruct((TAIL_ROWS, D_FF), f32),
                   jax.ShapeDtypeStruct((TAIL_ROWS, D_FF), f32)),
        grid=(nb, N_MB),
        in_specs=[pl.BlockSpec((TM, D), lambda n, m: (m, 0)), ws(0), ws(nb), cws(0), cws(nb), cbs(0), cbs(nb),
                  sts(0), sts(nb)],
        out_specs=(pl.BlockSpec((TM, TC_FF), lambda n, m: (m, n)),
                   pl.BlockSpec((TAIL_ROWS, TC_FF), lambda n, m: (0, n)),
                   pl.BlockSpec((TAIL_ROWS, TC_FF), lambda n, m: (0, n))),
        scratch_shapes=[pltpu.VMEM((D, TC_FF), bf16), pltpu.VMEM((D, TC_FF), bf16),
                        pltpu.VMEM((8, TC_FF), f32), pltpu.VMEM((8, TC_FF), f32)],
        compiler_params=_cparams(("arbitrary", "arbitrary")),
        name=f"up_act{layer}",
    )(h, w_up, w_up, conv_w, conv_w, cb3, cb3, conv_state, conv_state)


def _ffn_kernel(act_ref, wd_ref, xm_ref, g2_ref, gain_ref, sh_ref, sc_ref, xo_ref, ho_ref):
    xo_ref[...] = _dot(act_ref[...], wd_ref[...])
    g2 = g2_ref[...]
    gain = gain_ref[...]
    for s in range(TR_FF // SB):
        rows = slice(s * SB, (s + 1) * SB)
        xn = xm_ref[rows, :] + g2 * xo_ref[rows, :]
        xo_ref[rows, :] = xn
        ho_ref[rows, :] = (_rms(xn) * gain * (1.0 + sc_ref[...]) + sh_ref[...]).astype(bf16)


def _ffn_final_kernel(act_ref, wd_ref, xm_ref, g2_ref, gain_ref, yp_ref, ys_ref, y_scr):
    y_scr[...] = _dot(act_ref[...], wd_ref[...])
    g2 = g2_ref[...]
    gain = gain_ref[...]
    for s in range(TR_FF // SB):
        rows = slice(s * SB, (s + 1) * SB)
        y_scr[rows, :] = _rms(xm_ref[rows, :] + g2 * y_scr[rows, :]) * gain
    is_prompt = pl.program_id(0) < T_P // TR_FF

    @pl.when(is_prompt)
    def _():
        yp_ref[...] = y_scr[...]

    @pl.when(jnp.logical_not(is_prompt))
    def _():
        ys_ref[...] = y_scr[...]


def _ffn(act, wd_bf, x_mid, mods, gain, layer, final):
    row = lambda i: (i, 0)
    n_p = T_P // TR_FF
    in_specs = [
        pl.BlockSpec((TR_FF, D_FF), row),
        pl.BlockSpec((None, D_FF, D), lambda i: (layer, 0, 0), pipeline_mode=pl.Buffered(1)),
        pl.BlockSpec((TR_FF, D), row),
        _mod_spec(layer, _MOD_G2, TR_FF),
        pl.BlockSpec((1, D), lambda i: (0, 0)),
    ]
    if final:
        return pl.pallas_call(
            _ffn_final_kernel,
            out_shape=(jax.ShapeDtypeStruct((T_P, D), f32), jax.ShapeDtypeStruct((T_S, D), f32)),
            grid=(T // TR_FF,),
            in_specs=in_specs,
            out_specs=(pl.BlockSpec((TR_FF, D), lambda i: (jnp.minimum(i, n_p - 1), 0)),
                       pl.BlockSpec((TR_FF, D), lambda i: (jnp.maximum(i - n_p, 0), 0))),
            scratch_shapes=[pltpu.VMEM((TR_FF, D), f32)],
            compiler_params=_cparams(("arbitrary",)),
            name=f"ffn{layer}",
        )(act, wd_bf, x_mid, mods, gain.reshape(1, D))
    return pl.pallas_call(
        _ffn_kernel,
        out_shape=(jax.ShapeDtypeStruct((T, D), f32), jax.ShapeDtypeStruct((T, D), bf16)),
        grid=(T // TR_FF,),
        in_specs=in_specs + [_mod_spec(layer + 1, _MOD_SH1, TR_FF), _mod_spec(layer + 1, _MOD_SC1, TR_FF)],
        out_specs=(pl.BlockSpec((TR_FF, D), row), pl.BlockSpec((TR_FF, D), row)),
        compiler_params=_cparams(("arbitrary",)),
        name=f"ffn{layer}",
    )(act, wd_bf, x_mid, mods, gain.reshape(1, D), mods, mods)


def _to_rows(a):
    f = a.shape[-1]
    return a.reshape(N_SG, SB, DEC_SEQ, f).transpose(0, 2, 1, 3).reshape(T_S, f)


def _from_rows(a, steps=DEC_SEQ):
    f = a.shape[-1]
    return a.reshape(N_SG, steps, SB, f).transpose(0, 2, 1, 3).reshape(DEC_BATCH, steps, f)


def kernel(x_prompt, x_sample, c_prompt, c_sample, state_hgrn, state_ffn_conv, ada_w, ada_b, norm1_g, norm2_g,
           w_in, gmlp_v_g, gmlp_ws, gmlp_bs, w_branch_a, hgrn_lower_bounds, hgrn_norm_g, w_branch_b, w_out,
           w_up, conv_w, conv_b, w_down, final_norm_g):
    xs = (x_prompt.reshape(T_P, D), _to_rows(x_sample))
    c_pad = -(DEC_BATCH + BATCH) % SUBLANES
    c_all = jnp.concatenate([c_sample, c_prompt, jnp.zeros((c_pad, D), f32)], axis=0)
    mods = _ada(c_all, ada_w, ada_b)

    wout_bf = w_out.astype(bf16)
    wd_bf = w_down.astype(bf16)
    conv_st = state_ffn_conv.transpose(0, 2, 1, 3).reshape(DEPTH, CONV_W - 1, N_SG, SB, 2 * D_FF)

    h = _norm_mod(*xs, norm1_g[0], mods, 0)
    hs_all = None
    hp, cp, cs, vp, vs = [], [], [], [], []
    y_p = y_s = None
    for l in range(DEPTH):
        zu, zv, zqo, zf, zi, zg = _inproj(h, w_in, hgrn_lower_bounds, gmlp_v_g, l)
        a_gated = _gmlp(zu, zv, zg, gmlp_ws[l], gmlp_bs[l], w_branch_a, l)
        on_p, s_p = _hgrn_prompt(zqo, zf, zi, hgrn_norm_g, l)
        on_s, hs_all = _hgrn_sample(zqo, zf, zi, state_hgrn, hs_all, hgrn_norm_g, l)
        ymix = _branch_mix(on_p, on_s, w_branch_b, a_gated, zg, l)
        x_mid, h2 = _outproj(ymix, wout_bf, xs, norm2_g, mods, l)
        act, tail_a, tail_b = _up_act(h2, w_up, conv_w, conv_b, conv_st[l], l)
        if l == DEPTH - 1:
            y_p, y_s = _ffn(act, wd_bf, x_mid, mods, final_norm_g, l, True)
        else:
            x, h = _ffn(act, wd_bf, x_mid, mods, norm1_g[l + 1], l, False)
            xs = (x,)
        tail = jnp.concatenate([tail_a, tail_b], axis=1)
        hp.append(s_p)
        cp.append(tail[:SUBLANES * BATCH].reshape(BATCH, SUBLANES, 2 * D_FF)[:, SUBLANES - (CONV_W - 1):])
        cs.append(_from_rows(tail[TAIL_P:], CONV_W - 1))
        vp.append(jnp.stack([zv[(b + 1) * SEQ - CHUNK:(b + 1) * SEQ] for b in range(BATCH)]))
        vs.append(_from_rows(zv[T_P:]))

    y_prompt = y_p.reshape(BATCH, SEQ, D)
    y_sample = _from_rows(y_s)
    return (y_prompt, y_sample, jnp.stack(hp), hs_all, jnp.stack(cp), jnp.stack(cs), jnp.stack(vp), jnp.stack(vs))
```

```python
import functools

import numpy as np
import jax
import jax.numpy as jnp
from jax import lax
from jax.experimental import pallas as pl
from jax.experimental.pallas import tpu as pltpu

f32 = jnp.float32
bf16 = jnp.bfloat16

D = 2048
BATCH = 4
SEQ = 2048
DEPTH = 2
DEC_BATCH = 128
DEC_SEQ = 8
G_A = 8
DG_A = 128
D_A = G_A * DG_A
CHUNK = 128
H_B = 16
DK = 128
DV = 128
D_FF = 5632
CONV_W = 3
N_IN = 2 * D_A + 6 * D
EPS = 1e-6

T_P = BATCH * SEQ
T_S = DEC_BATCH * DEC_SEQ
T = T_P + T_S
SB = 64
N_SG = DEC_BATCH // SB
TR = DEC_SEQ * SB
N_RB = T // TR
N_PRB = T_P // TR
RB_PER_SEQ = SEQ // TR
TM = 1024
N_MB = T // TM
N_PMB = T_P // TM
MB_PER_SEQ = SEQ // TM
TN = 1024
TC_FF = 512
N_CFF = D_FF // TC_FF
TR_FF = 256
HG_C = 128
HG_NH = 8
HG_BLOCK_LEVELS = (64, 32, 16)
HG_MASK_LEVELS = (8, 4, 2, 1)
HS_UNROLL = 32
HS_NBUF = 3
SUBLANES = 8
LOG2E = 1.4426950408889634
TAIL_P = 128
TAIL_ROWS = TAIL_P + (CONV_W - 1) * DEC_BATCH
assert BATCH * SUBLANES <= TAIL_P

_U0, _V0, _Q0, _F0, _I0, _OG0, _GA0 = 0, 1, 2, 4, 6, 8, 10
_MOD_SH1, _MOD_SC1, _MOD_G1, _MOD_SH2, _MOD_SC2, _MOD_G2 = range(6)
N_MODJ = BATCH + N_SG


def _cparams(sem):
    return pltpu.CompilerParams(dimension_semantics=sem)


def _hgrn_level_table(c):
    t = np.arange(c)
    lev = np.full((c, c), -1, np.int32)
    lev[t, t] = 0
    for h in HG_MASK_LEVELS:
        upper = (t % (2 * h)) >= h
        same = (t[:, None] // (2 * h)) == (t[None, :] // (2 * h))
        lev[same & upper[:, None] & ~upper[None, :]] = h
    return lev


_LEV_NP = _hgrn_level_table(HG_C)


def _rms(x, eps=EPS):
    return x * lax.rsqrt(jnp.mean(x * x, axis=-1, keepdims=True) + eps)


def _dot(a, b):
    return jnp.dot(a, b, preferred_element_type=f32)


def _dot_nt(a, b):
    return lax.dot_general(a, b, (((1,), (1,)), ((), ())), preferred_element_type=f32)


def _dot_tn(a, b):
    return lax.dot_general(a, b, (((0,), (0,)), ((), ())), preferred_element_type=f32)


def _modj(i, rows):
    return jnp.where(i < T_P // rows, i // (SEQ // rows), BATCH + (i - T_P // rows) // (TR // rows))


def _mod_spec(layer, kind, rows=TR):
    return pl.BlockSpec((None, None, None, SB, D), lambda i, *_: (layer, kind, _modj(i, rows), 0, 0))


def _ada_kernel(c_ref, w_ref, b_ref, o_ref):
    c = c_ref[...]
    a = (c * jax.nn.sigmoid(c)).astype(bf16)
    r = _dot(a, w_ref[...].astype(bf16)) + b_ref[...]
    for g in range(N_SG):
        o_ref[BATCH + g] = r[g * SB:(g + 1) * SB]
    for j in range(BATCH):
        o_ref[j] = jnp.broadcast_to(r[DEC_BATCH + j:DEC_BATCH + j + 1], (SB, TN))


def _ada(c_all, ada_w, ada_b):
    nrow = c_all.shape[0]
    ncol = 6 * D // TN
    per_kind = D // TN
    return pl.pallas_call(
        _ada_kernel,
        out_shape=jax.ShapeDtypeStruct((DEPTH, 6, N_MODJ, SB, D), f32),
        grid=(DEPTH, ncol),
        in_specs=[
            pl.BlockSpec((nrow, D), lambda l, n: (0, 0)),
            pl.BlockSpec((None, D, TN), lambda l, n: (l, 0, n)),
            pl.BlockSpec((None, 1, TN), lambda l, n: (l, 0, n)),
        ],
        out_specs=pl.BlockSpec((None, None, N_MODJ, SB, TN), lambda l, n: (l, n // per_kind, 0, 0, n % per_kind)),
        compiler_params=_cparams(("arbitrary", "arbitrary")),
        name="ada_mod",
    )(c_all, ada_w, ada_b.reshape(DEPTH, 1, 6 * D))


def _x_specs(rows):
    n_p = T_P // rows
    return [pl.BlockSpec((rows, D), lambda i: (jnp.minimum(i, n_p - 1), 0)),
            pl.BlockSpec((rows, D), lambda i: (jnp.maximum(i - n_p, 0), 0))]


def _norm_mod_kernel(xp_ref, xs_ref, g_ref, sh_ref, sc_ref, o_ref):
    g = g_ref[...]
    sh = sh_ref[...]
    sc1 = 1.0 + sc_ref[...]
    is_prompt = pl.program_id(0) < N_PRB
    for s in range(TR // SB):
        rows = slice(s * SB, (s + 1) * SB)
        x = jnp.where(is_prompt, xp_ref[rows, :], xs_ref[rows, :])
        o_ref[rows, :] = (_rms(x) * g * sc1 + sh).astype(bf16)


def _norm_mod(xp, xs, gain, mods, layer):
    return pl.pallas_call(
        _norm_mod_kernel,
        out_shape=jax.ShapeDtypeStruct((T, D), bf16),
        grid=(N_RB,),
        in_specs=[
            *_x_specs(TR),
            pl.BlockSpec((1, D), lambda i: (0, 0)),
            _mod_spec(layer, _MOD_SH1),
            _mod_spec(layer, _MOD_SC1),
        ],
        out_specs=pl.BlockSpec((TR, D), lambda i: (i, 0)),
        compiler_params=_cparams(("arbitrary",)),
        name="norm_mod",
    )(xp, xs, gain.reshape(1, D), mods, mods)


def _proj_kernel(h_ref, w_ref, *rest, epilogue):
    *extra, o_ref, wbf_ref = rest

    @pl.when(pl.program_id(1) == 0)
    def _():
        wbf_ref[...] = w_ref[...].astype(bf16)

    epilogue(_dot(h_ref[...], wbf_ref[...]), o_ref, *extra)


def _epi_gelu(acc, o_ref):
    o_ref[...] = jax.nn.gelu(acc).astype(o_ref.dtype)


def _epi_v(acc, o_ref, vg_ref):
    v = jax.nn.gelu(acc)
    for g in range(G_A):
        cols = slice(g * DG_A, (g + 1) * DG_A)
        o_ref[:, cols] = _rms(v[:, cols]) * vg_ref[:, cols]


def _epi_silu(acc, o_ref, scale_ref):
    o_ref[...] = (acc * jax.nn.sigmoid(acc) * scale_ref[pl.program_id(0)]).astype(o_ref.dtype)


def _epi_logf(acc, o_ref, lb_ref, *, layer):
    p = jax.nn.softmax(lb_ref[...], axis=0)
    lb = jnp.zeros((1, TN), f32)
    for j in range(1, layer + 1):
        lb = lb + p[j:j + 1]
    a = jnp.log(lb)
    b = jnp.log1p(-lb) + jnp.minimum(acc, 0.0) - jnp.log(1.0 + jnp.exp(-jnp.abs(acc)))
    o_ref[...] = jnp.maximum(a, b) + jnp.log(1.0 + jnp.exp(-jnp.abs(a - b)))


def _epi_id(acc, o_ref):
    o_ref[...] = acc.astype(o_ref.dtype)


def _epi_sigmoid(acc, o_ref):
    o_ref[...] = jax.nn.sigmoid(acc).astype(o_ref.dtype)


def _proj(name, h, w_in, layer, col_of, n_tiles, out_dtype, epilogue, extra=(), extra_specs=()):
    return pl.pallas_call(
        functools.partial(_proj_kernel, epilogue=epilogue),
        out_shape=jax.ShapeDtypeStruct((T, n_tiles * TN), out_dtype),
        grid=(n_tiles, N_MB),
        in_specs=[
            pl.BlockSpec((TM, D), lambda n, m: (m, 0)),
            pl.BlockSpec((None, D, TN), lambda n, m: (layer, 0, col_of(n))),
            *extra_specs,
        ],
        out_specs=pl.BlockSpec((TM, TN), lambda n, m: (m, n)),
        scratch_shapes=[pltpu.VMEM((D, TN), bf16)],
        compiler_params=_cparams(("arbitrary", "arbitrary")),
        name=f"{name}{layer}",
    )(h, w_in, *extra)


def _inproj(h, w_in, lower_bounds, v_gain, layer):
    smem = pl.BlockSpec(memory_space=pltpu.SMEM)
    nd = D // TN
    zu = _proj("in_u", h, w_in, layer, lambda n: _U0 + n, D_A // TN, bf16, _epi_gelu)
    zv = _proj("in_v", h, w_in, layer, lambda n: _V0 + n, D_A // TN, f32, _epi_v,
               (v_gain.reshape(DEPTH, 1, D_A),), (pl.BlockSpec((None, 1, D_A), lambda n, m: (layer, 0, 0)),))
    scales = jnp.asarray([DK ** -0.5] * nd + [1.0] * nd, f32)
    zqo = _proj("in_qo", h, w_in, layer, lambda n: jnp.where(n < nd, _Q0 + n, _OG0 + n - nd), 2 * nd, bf16, _epi_silu,
                (scales,), (smem,))
    zf = _proj("in_f", h, w_in, layer, lambda n: _F0 + n, nd, f32, functools.partial(_epi_logf, layer=layer),
               (lower_bounds,), (pl.BlockSpec((DEPTH, TN), lambda n, m: (0, n)),))
    zi = _proj("in_i", h, w_in, layer, lambda n: _I0 + n, nd, bf16, _epi_id)
    zg = _proj("in_g", h, w_in, layer, lambda n: _GA0 + n, 2 * nd, bf16, _epi_sigmoid)
    return zu, zv, zqo, zf, zi, zg


def _gmlp_kernel(u_ref, v_ref, ga_ref, ws_ref, bias_ref, w8_ref, b8_ref, wba_ref, o_ref, wm_ref, ya_ref, wbf_ref):
    i = pl.program_id(0)

    @pl.when(i == 0)
    def _():
        wbf_ref[...] = wba_ref[...].astype(bf16)
        r = lax.broadcasted_iota(jnp.int32, (CHUNK, CHUNK), 0)
        c = lax.broadcasted_iota(jnp.int32, (CHUNK, CHUNK), 1)
        for g in range(G_A):
            wm_ref[g] = jnp.where(r >= c, ws_ref[g], 0.0).astype(bf16)

    @pl.when(i < N_PRB)
    def _():
        for ch in range(TR // CHUNK):
            rows = slice(ch * CHUNK, (ch + 1) * CHUNK)
            for g in range(G_A):
                cols = slice(g * DG_A, (g + 1) * DG_A)
                s = _dot(wm_ref[g], v_ref[rows, cols].astype(bf16)) + bias_ref[:, cols]
                ya_ref[rows, cols] = (u_ref[rows, cols].astype(f32) * s).astype(bf16)

    @pl.when(i >= N_PRB)
    def _():
        for g in range(G_A):
            cols = slice(g * DG_A, (g + 1) * DG_A)
            vs = [v_ref[s * SB:(s + 1) * SB, cols] for s in range(DEC_SEQ)]
            for t in range(DEC_SEQ):
                w0 = (g * DEC_SEQ + t) * DEC_SEQ
                acc = vs[0] * w8_ref[w0]
                for s in range(1, t + 1):
                    acc = acc + vs[s] * w8_ref[w0 + s]
                acc = acc + b8_ref[g * DEC_SEQ + t]
                ya_ref[t * SB:(t + 1) * SB, cols] = (u_ref[t * SB:(t + 1) * SB, cols].astype(f32) * acc).astype(bf16)

    o_ref[...] = (ga_ref[...].astype(f32) * _dot(ya_ref[...], wbf_ref[...])).astype(bf16)


def _gmlp(zu, zv, zg, ws, bs, w_ba, layer):
    bias_full = jnp.repeat(bs.T, DG_A, axis=1)
    w8 = ws[:, :DEC_SEQ, :DEC_SEQ].reshape(G_A * DEC_SEQ * DEC_SEQ)
    b8 = bs[:, :DEC_SEQ].reshape(G_A * DEC_SEQ)
    return pl.pallas_call(
        _gmlp_kernel,
        out_shape=jax.ShapeDtypeStruct((T, D), bf16),
        grid=(N_RB,),
        in_specs=[
            pl.BlockSpec((TR, D_A), lambda i: (i, 0)),
            pl.BlockSpec((TR, D_A), lambda i: (i, 0)),
            pl.BlockSpec((TR, D), lambda i: (i, 0)),
            pl.BlockSpec((G_A, CHUNK, CHUNK), lambda i: (0, 0, 0)),
            pl.BlockSpec((CHUNK, D_A), lambda i: (0, 0)),
            pl.BlockSpec(memory_space=pltpu.SMEM),
            pl.BlockSpec(memory_space=pltpu.SMEM),
            pl.BlockSpec((None, D_A, D), lambda i: (layer, 0, 0), pipeline_mode=pl.Buffered(1)),
        ],
        out_specs=pl.BlockSpec((TR, D), lambda i: (i, 0)),
        scratch_shapes=[pltpu.VMEM((G_A, CHUNK, CHUNK), bf16), pltpu.VMEM((TR, D_A), bf16),
                        pltpu.VMEM((D_A, D), bf16)],
        compiler_params=_cparams(("arbitrary",)),
        name=f"gmlp{layer}",
    )(zu, zv, zg, ws, bias_full, w8, b8, w_ba)


def _bcast_row(x8, r):
    return jnp.broadcast_to(x8[r:r + 1, :], x8.shape)


def _cat_rows(tiles):
    return tiles[0] if len(tiles) == 1 else jnp.concatenate(tiles, axis=0)


def _hgrn_chunk(q, g, v, sts, lev, r8, nh):
    c = HG_C
    nv = c // 8
    w = nh * DK
    hs = [slice(i * DK, (i + 1) * DK) for i in range(nh)]
    gl = g * LOG2E
    x_t = []
    for j in range(nv):
        x = gl[8 * j:8 * j + 8, :]
        for sh in (1, 2, 4):
            x = x + jnp.where(r8 >= sh, pltpu.roll(x, sh, 0), 0.0)
        x_t.append(x)
    last = [_bcast_row(x, 7) for x in x_t]
    sh = 1
    while sh < nv:
        last = [last[j] + last[j - sh] if j >= sh else last[j] for j in range(nv)]
        sh *= 2
    a_t = [x_t[0]] + [x_t[j] + last[j - 1] for j in range(1, nv)]
    f = jnp.exp2(gl)
    kk = 1.0 - f
    q_t = [q[8 * j:8 * j + 8, :] for j in range(nv)]
    k_t = [kk[8 * j:8 * j + 8, :] for j in range(nv)]
    a = _cat_rows(a_t)
    odd = (lax.broadcasted_iota(jnp.int32, (c, w), 0) & 1) == 1

    qb = q.astype(bf16)
    kb0 = kk.astype(bf16)
    sc = [jnp.where(lev == 0, _dot_nt(qb[:, hh], kb0[:, hh]), 0.0) for hh in hs]
    for h in HG_MASK_LEVELS:
        if h == 8:
            e = _cat_rows([(a_t[j] - last[j - 1]) if j % 2 else (last[j] - a_t[j]) for j in range(nv)])
        elif h == 4:
            d = _cat_rows([t - _bcast_row(t, 3) for t in a_t])
            e = jnp.minimum(d, -d)
        elif h == 2:
            d = _cat_rows([t - jnp.where(r8 < 4, _bcast_row(t, 1), _bcast_row(t, 5)) for t in a_t])
            e = jnp.minimum(d, -d)
        if h >= 2:
            e = jnp.exp2(e)
        else:
            e = jnp.where(odd, f, 1.0)
        qe = (q * e).astype(bf16)
        ke = (kk * e).astype(bf16)
        sc = [jnp.where(lev == h, _dot_nt(qe[:, hh], ke[:, hh]), sc[i]) for i, hh in enumerate(hs)]
    sc_t = [[s_[8 * j:8 * j + 8, :] for j in range(nv)] for s_ in sc]
    zero8 = jnp.zeros((8, w), f32)
    for h in HG_BLOCK_LEVELS:
        ht = h // 8
        for blk in range(c // (2 * h)):
            lo = range(blk * 2 * ht, blk * 2 * ht + ht)
            up = range(blk * 2 * ht + ht, (blk + 1) * 2 * ht)
            ref = last[blk * 2 * ht + ht - 1]
            qt = _cat_rows([q_t[j] * jnp.exp2(a_t[j] - ref) for j in up]).astype(bf16)
            kz = _cat_rows([k_t[j] * jnp.exp2(ref - a_t[j]) if j in lo else zero8 for j in range(nv)]).astype(bf16)
            for i, hh in enumerate(hs):
                s_ = _dot_nt(qt[:, hh], kz[:, hh])
                for r, j in enumerate(up):
                    sc_t[i][j] = sc_t[i][j] + s_[8 * r:8 * r + 8, :]
    qa = (q * jnp.exp2(a)).astype(bf16)
    a_last = last[-1]
    kb = _cat_rows([k_t[j] * jnp.exp2(a_last - a_t[j]) for j in range(nv)]).astype(bf16)
    fl = jnp.exp2(a_last[0:1, :])
    o, st_new = [], []
    for i, hh in enumerate(hs):
        o.append(_dot_nt(qa[:, hh], sts[i].astype(bf16)) + _dot(_cat_rows(sc_t[i]).astype(bf16), v[:, hh]))
        st_new.append(fl[:, hh] * sts[i] + _dot_tn(v[:, hh], kb[:, hh]))
    return o, st_new


def _hgrn_prompt_kernel(q_ref, g_ref, v_ref, og_ref, lev_ref, gain_ref, on_ref, so_ref, st_ref):
    r8 = lax.broadcasted_iota(jnp.int32, (8, HG_NH * DK), 0)
    for hh in range(HG_NH):
        st_ref[hh] = jnp.zeros((DV, DK), f32)

    def chunk(c, carry):
        r = pl.ds(pl.multiple_of(c * HG_C, HG_C), HG_C)
        o, st_new = _hgrn_chunk(q_ref[r, :].astype(f32), g_ref[r, :], v_ref[r, :],
                                [st_ref[hh] for hh in range(HG_NH)], lev_ref[...], r8, HG_NH)
        for hh in range(HG_NH):
            cols = slice(hh * DV, (hh + 1) * DV)
            on_ref[r, cols] = (_rms(o[hh]) * gain_ref[:, cols] * og_ref[r, cols].astype(f32)).astype(bf16)
            st_ref[hh] = st_new[hh]
        return carry

    lax.fori_loop(0, SEQ // HG_C, chunk, 0)
    for hh in range(HG_NH):
        so_ref[hh] = st_ref[hh].T


def _hgrn_prompt(zqo, zf, zi, gain, layer):
    ng = H_B // HG_NH
    spec = lambda off: pl.BlockSpec((SEQ, HG_NH * DK), lambda b, h: (b, off + h))
    return pl.pallas_call(
        _hgrn_prompt_kernel,
        out_shape=(jax.ShapeDtypeStruct((T_P, D), bf16), jax.ShapeDtypeStruct((BATCH, H_B, DK, DV), f32)),
        grid=(BATCH, ng),
        in_specs=[
            spec(0), spec(0), spec(0), spec(ng),
            pl.BlockSpec((HG_C, HG_C), lambda b, h: (0, 0)),
            pl.BlockSpec((None, 1, HG_NH * DV), lambda b, h: (layer, 0, h)),
        ],
        out_specs=(pl.BlockSpec((SEQ, HG_NH * DV), lambda b, h: (b, h)),
                   pl.BlockSpec((None, HG_NH, DK, DV), lambda b, h: (b, h, 0, 0))),
        scratch_shapes=[pltpu.VMEM((HG_NH, DV, DK), f32)],
        compiler_params=_cparams(("arbitrary", "arbitrary")),
        name=f"hgrn_prompt{layer}",
    )(zqo, zf, zi, zqo, jnp.asarray(_LEV_NP), gain.reshape(DEPTH, 1, D))


def _hgrn_sample_kernel(*refs, layer):
    q_ref, g_ref, v_ref, og_ref, s_hbm, gain_ref = refs[:6]
    on_ref, so_ref, qa_scr, kb_scr, v_scr, fs_scr, o_scr, s_buf, s_sem = refs[-9:]
    slab = lambda t: slice(t * SB, (t + 1) * SB)
    step = pl.program_id(0) * H_B + pl.program_id(1)
    n_steps = N_SG * H_B

    def state_copy(t, slot):
        src = s_hbm.at[layer, pl.ds((t // H_B) * SB, SB), t % H_B]
        return pltpu.make_async_copy(src, s_buf.at[slot], s_sem.at[slot])

    @pl.when(step == 0)
    def _():
        for t in range(HS_NBUF - 1):
            state_copy(t, t).start()

    ahead = step + (HS_NBUF - 1)

    @pl.when(ahead < n_steps)
    def _():
        state_copy(ahead, ahead % HS_NBUF).start()

    slot = step % HS_NBUF
    a, q, kk, v = [], [], [], []
    for t in range(DEC_SEQ):
        gl = g_ref[slab(t), :] * LOG2E
        a.append(gl if t == 0 else a[t - 1] + gl)
        kk.append(1.0 - jnp.exp2(gl))
        q.append(q_ref[slab(t), :].astype(f32))
        v.append(v_ref[slab(t), :].astype(f32))
    a_last = a[DEC_SEQ - 1]
    for t in range(DEC_SEQ):
        qa_scr[slab(t), :] = q[t] * jnp.exp2(a[t])
        kb_scr[slab(t), :] = kk[t] * jnp.exp2(a_last - a[t])
        v_scr[slab(t), :] = v[t]
        o = jnp.sum(q[t] * kk[t], axis=-1, keepdims=True) * v[t]
        for s in range(t):
            o = o + jnp.sum(q[t] * kk[s] * jnp.exp2(a[t] - a[s]), axis=-1, keepdims=True) * v[s]
        o_scr[slab(t), :] = o
    fl = jnp.exp2(a_last)
    f1 = fl.astype(bf16).astype(f32)
    r1 = fl - f1
    f2 = r1.astype(bf16).astype(f32)
    fs_scr[slab(0), :] = f1
    fs_scr[slab(1), :] = f2
    fs_scr[slab(2), :] = r1 - f2
    fs_scr[3 * SB:, :] = jnp.zeros((TR - 3 * SB, DK), f32)

    ones3 = (lax.broadcasted_iota(jnp.int32, (DEC_SEQ, DV), 0) < 3).astype(bf16)
    zero8 = jnp.zeros((DEC_SEQ, DV), bf16)

    state_copy(step, slot).wait()

    def per_batch(b, carry):
        rows = pl.ds(b, DEC_SEQ, stride=SB)
        s0 = s_buf[slot, b]
        o_scr[rows, :] = o_scr[rows, :] + _dot(qa_scr[rows, :].astype(bf16), s0.astype(bf16))
        lhs = jnp.concatenate([kb_scr[rows, :], fs_scr[rows, :]], axis=0).astype(bf16)
        v8 = v_scr[rows, :].astype(bf16)
        rhs = jnp.concatenate([jnp.concatenate([v8, zero8], axis=1), jnp.concatenate([zero8, ones3], axis=1)], axis=0)
        r = _dot_tn(lhs, rhs)
        so_ref[b] = r[:, DV:] * s0 + r[:, :DV]
        return carry

    lax.fori_loop(0, SB, per_batch, 0, unroll=HS_UNROLL)
    gain = gain_ref[...]
    for t in range(DEC_SEQ):
        on_ref[slab(t), :] = (_rms(o_scr[slab(t), :]) * gain * og_ref[slab(t), :].astype(f32)).astype(bf16)


def _hgrn_sample(zqo, zf, zi, state_hgrn, hs_prev, gain, layer):
    rb0 = T_P // TR
    spec = lambda off: pl.BlockSpec((TR, DK), lambda g, h: (rb0 + g, off + h))
    sspec = pl.BlockSpec((None, SB, None, DK, DV), lambda g, h: (layer, g, h, 0, 0))
    ins = [zqo, zf, zi, zqo, state_hgrn, gain.reshape(DEPTH, 1, D)]
    in_specs = [spec(0), spec(0), spec(0), spec(H_B), pl.BlockSpec(memory_space=pl.ANY),
                pl.BlockSpec((None, 1, DV), lambda g, h: (layer, 0, h))]
    aliases = {}
    if hs_prev is not None:
        ins.append(hs_prev)
        in_specs.append(pl.BlockSpec(memory_space=pl.ANY))
        aliases[6] = 1
    return pl.pallas_call(
        functools.partial(_hgrn_sample_kernel, layer=layer),
        out_shape=(jax.ShapeDtypeStruct((T_S, D), bf16), jax.ShapeDtypeStruct(state_hgrn.shape, f32)),
        grid=(N_SG, H_B),
        in_specs=in_specs,
        out_specs=(pl.BlockSpec((TR, DV), lambda g, h: (g, h)), sspec),
        scratch_shapes=[pltpu.VMEM((TR, DV), f32)] * 5 + [pltpu.VMEM((HS_NBUF, SB, DK, DV), f32),
                                                          pltpu.SemaphoreType.DMA((HS_NBUF,))],
        input_output_aliases=aliases,
        compiler_params=_cparams(("arbitrary", "arbitrary")),
        name=f"hgrn_sample{layer}",
    )(*ins)


def _mix_kernel(onp_ref, ons_ref, w_ref, a_ref, gb_ref, o_ref, wbf_ref):
    m = pl.program_id(1)

    @pl.when(m == 0)
    def _():
        wbf_ref[...] = w_ref[...].astype(bf16)

    on = jnp.where(m < N_PMB, onp_ref[...], ons_ref[...])
    o_ref[...] = (a_ref[...].astype(f32) + gb_ref[...].astype(f32) * _dot(on, wbf_ref[...])).astype(bf16)


def _branch_mix(on_p, on_s, w_bb, a_gated, zg, layer):
    return pl.pallas_call(
        _mix_kernel,
        out_shape=jax.ShapeDtypeStruct((T, D), bf16),
        grid=(D // TN, N_MB),
        in_specs=[
            pl.BlockSpec((TM, D), lambda n, m: (jnp.minimum(m, N_PMB - 1), 0)),
            pl.BlockSpec((TM, D), lambda n, m: (jnp.maximum(m - N_PMB, 0), 0), pipeline_mode=pl.Buffered(1)),
            pl.BlockSpec((None, D, TN), lambda n, m: (layer, 0, n)),
            pl.BlockSpec((TM, TN), lambda n, m: (m, n)),
            pl.BlockSpec((TM, TN), lambda n, m: (m, D // TN + n)),
        ],
        out_specs=pl.BlockSpec((TM, TN), lambda n, m: (m, n)),
        scratch_shapes=[pltpu.VMEM((D, TN), bf16)],
        compiler_params=_cparams(("arbitrary", "arbitrary")),
        name=f"branch_mix{layer}",
    )(on_p, on_s, w_bb, a_gated, zg)


def _outproj_kernel(y_ref, w_ref, *rest, split_x):
    *x_refs, n2_ref, g1_ref, sh_ref, sc_ref, xo_ref, h_ref = rest
    xo_ref[...] = _dot(y_ref[...], w_ref[...])
    n2 = n2_ref[...]
    g1 = g1_ref[...]
    sh = sh_ref[...]
    sc1 = 1.0 + sc_ref[...]
    is_prompt = pl.program_id(0) < N_PRB
    for s in range(TR // SB):
        rows = slice(s * SB, (s + 1) * SB)
        x = jnp.where(is_prompt, x_refs[0][rows, :], x_refs[1][rows, :]) if split_x else x_refs[0][rows, :]
        xm = x + g1 * xo_ref[rows, :]
        xo_ref[rows, :] = xm
        h_ref[rows, :] = (_rms(xm) * n2 * sc1 + sh).astype(bf16)


def _outproj(y, w_out_bf, xs, n2, mods, layer):
    split_x = len(xs) == 2
    return pl.pallas_call(
        functools.partial(_outproj_kernel, split_x=split_x),
        out_shape=(jax.ShapeDtypeStruct((T, D), f32), jax.ShapeDtypeStruct((T, D), bf16)),
        grid=(N_RB,),
        in_specs=[
            pl.BlockSpec((TR, D), lambda i: (i, 0)),
            pl.BlockSpec((None, D, D), lambda i: (layer, 0, 0), pipeline_mode=pl.Buffered(1)),
            *(_x_specs(TR) if split_x else [pl.BlockSpec((TR, D), lambda i: (i, 0))]),
            pl.BlockSpec((None, 1, D), lambda i: (layer, 0, 0)),
            _mod_spec(layer, _MOD_G1), _mod_spec(layer, _MOD_SH2), _mod_spec(layer, _MOD_SC2),
        ],
        out_specs=(pl.BlockSpec((TR, D), lambda i: (i, 0)), pl.BlockSpec((TR, D), lambda i: (i, 0))),
        compiler_params=_cparams(("arbitrary",)),
        name=f"outproj{layer}",
    )(y, w_out_bf, *xs, n2.reshape(DEPTH, 1, D), mods, mods, mods)


def _conv3(cur, m1, m2, cw_ref, cb_ref):
    return cb_ref[...] + cw_ref[0:1, :] * m2 + cw_ref[1:2, :] * m1 + cw_ref[2:3, :] * cur


def _up_kernel(h_ref, wa_ref, wb_ref, cwa_ref, cwb_ref, cba_ref, cbb_ref, sa_ref, sb_ref,
               act_ref, ta_ref, tb_ref, wabf_ref, wbbf_ref, haloa_ref, halob_ref):
    m = pl.program_id(1)

    @pl.when(m == 0)
    def _():
        wabf_ref[...] = wa_ref[...].astype(bf16)
        wbbf_ref[...] = wb_ref[...].astype(bf16)
        ta_ref[...] = jnp.zeros_like(ta_ref)
        tb_ref[...] = jnp.zeros_like(tb_ref)
        haloa_ref[...] = jnp.zeros_like(haloa_ref)
        halob_ref[...] = jnp.zeros_like(halob_ref)

    h = h_ref[...]
    acc_a = _dot(h, wabf_ref[...])
    acc_b = _dot(h, wbbf_ref[...])
    seq_start = (m % MB_PER_SEQ) == 0
    r8 = lax.broadcasted_iota(jnp.int32, (8, TC_FF), 0)

    def conv(cur, halo_ref, cw_ref, cb_ref):
        body = _conv3(cur, pltpu.roll(cur, 1, 0), pltpu.roll(cur, 2, 0), cw_ref, cb_ref)
        halo = jnp.where(seq_start, 0.0, halo_ref[...])
        cur8 = cur[0:8, :]
        m1 = jnp.where(r8 < 1, pltpu.roll(halo, 1, 0), pltpu.roll(cur8, 1, 0))
        m2 = jnp.where(r8 < 2, pltpu.roll(halo, 2, 0), pltpu.roll(cur8, 2, 0))
        return body, _conv3(cur8, m1, m2, cw_ref, cb_ref)

    ca, ca8 = conv(acc_a, haloa_ref, cwa_ref, cba_ref)
    cb, cb8 = conv(acc_b, halob_ref, cwb_ref, cbb_ref)
    act_ref[...] = (jax.nn.gelu(ca) * cb).astype(bf16)
    act_ref[0:8, :] = (jax.nn.gelu(ca8) * cb8).astype(bf16)
    ta8 = acc_a[TM - 8:TM, :]
    tb8 = acc_b[TM - 8:TM, :]
    haloa_ref[...] = ta8
    halob_ref[...] = tb8

    @pl.when((m < N_PMB) & (m % MB_PER_SEQ == MB_PER_SEQ - 1))
    def _():
        r = pl.ds(pl.multiple_of((m // MB_PER_SEQ) * SUBLANES, SUBLANES), SUBLANES)
        ta_ref[r, :] = ta8
        tb_ref[r, :] = tb8

    @pl.when(m == N_PMB)
    def _():
        def conv_s(acc, st_ref, cw_ref, cb_ref, g, t):
            slab = lambda j: acc[(g * DEC_SEQ + j) * SB:(g * DEC_SEQ + j + 1) * SB, :]
            m1 = slab(t - 1) if t >= 1 else st_ref[1, g]
            m2 = slab(t - 2) if t >= 2 else st_ref[t, g]
            return _conv3(slab(t), m1, m2, cw_ref, cb_ref)

        for g in range(N_SG):
            for t in range(DEC_SEQ):
                ca_ = conv_s(acc_a, sa_ref, cwa_ref, cba_ref, g, t)
                cb_ = conv_s(acc_b, sb_ref, cwb_ref, cbb_ref, g, t)
                r0 = (g * DEC_SEQ + t) * SB
                act_ref[r0:r0 + SB, :] = (jax.nn.gelu(ca_) * cb_).astype(bf16)
            for j in range(CONV_W - 1):
                r0 = (g * DEC_SEQ + DEC_SEQ - (CONV_W - 1) + j) * SB
                d0 = TAIL_P + (g * (CONV_W - 1) + j) * SB
                ta_ref[d0:d0 + SB, :] = acc_a[r0:r0 + SB, :]
                tb_ref[d0:d0 + SB, :] = acc_b[r0:r0 + SB, :]


def _up_act(h, w_up, conv_w, conv_b, conv_state, layer):
    nb = N_CFF
    cws = lambda off: pl.BlockSpec((None, CONV_W, TC_FF), lambda n, m: (layer, 0, off + n))
    cbs = lambda off: pl.BlockSpec((None, 1, TC_FF), lambda n, m: (layer, 0, off + n))
    sts = lambda off: pl.BlockSpec((CONV_W - 1, N_SG, SB, TC_FF), lambda n, m: (0, 0, 0, off + n))
    ws = lambda off: pl.BlockSpec((None, D, TC_FF), lambda n, m: (layer, 0, off + n))
    cb3 = conv_b.reshape(DEPTH, 1, 2 * D_FF)
    return pl.pallas_call(
        _up_kernel,
        out_shape=(jax.ShapeDtypeStruct((T, D_FF), bf16), jax.ShapeDtypeStruct((TAIL_ROWS, D_FF), f32),
                   jax.ShapeDtypeStruct((TAIL_ROWS, D_FF), f32)),
        grid=(nb, N_MB),
        in_specs=[pl.BlockSpec((TM, D), lambda n, m: (m, 0)), ws(0), ws(nb), cws(0), cws(nb), cbs(0), cbs(nb),
                  sts(0), sts(nb)],
        out_specs=(pl.BlockSpec((TM, TC_FF), lambda n, m: (m, n)),
                   pl.BlockSpec((TAIL_ROWS, TC_FF), lambda n, m: (0, n)),
                   pl.BlockSpec((TAIL_ROWS, TC_FF), lambda n, m: (0, n))),
        scratch_shapes=[pltpu.VMEM((D, TC_FF), bf16), pltpu.VMEM((D, TC_FF), bf16),
                        pltpu.VMEM((8, TC_FF), f32), pltpu.VMEM((8, TC_FF), f32)],
        compiler_params=_cparams(("arbitrary", "arbitrary")),
        name=f"up_act{layer}",
    )(h, w_up, w_up, conv_w, conv_w, cb3, cb3, conv_state, conv_state)


def _ffn_kernel(act_ref, wd_ref, xm_ref, g2_ref, gain_ref, sh_ref, sc_ref, xo_ref, ho_ref):
    xo_ref[...] = _dot(act_ref[...], wd_ref[...])
    g2 = g2_ref[...]
    gain = gain_ref[...]
    for s in range(TR_FF // SB):
        rows = slice(s * SB, (s + 1) * SB)
        xn = xm_ref[rows, :] + g2 * xo_ref[rows, :]
        xo_ref[rows, :] = xn
        ho_ref[rows, :] = (_rms(xn) * gain * (1.0 + sc_ref[...]) + sh_ref[...]).astype(bf16)


def _ffn_final_kernel(act_ref, wd_ref, xm_ref, g2_ref, gain_ref, yp_ref, ys_ref, y_scr):
    y_scr[...] = _dot(act_ref[...], wd_ref[...])
    g2 = g2_ref[...]
    gain = gain_ref[...]
    for s in range(TR_FF // SB):
        rows = slice(s * SB, (s + 1) * SB)
        y_scr[rows, :] = _rms(xm_ref[rows, :] + g2 * y_scr[rows, :]) * gain
    is_prompt = pl.program_id(0) < T_P // TR_FF

    @pl.when(is_prompt)
    def _():
        yp_ref[...] = y_scr[...]

    @pl.when(jnp.logical_not(is_prompt))
    def _():
        ys_ref[...] = y_scr[...]


def _ffn(act, wd_bf, x_mid, mods, gain, layer, final):
    row = lambda i: (i, 0)
    n_p = T_P // TR_FF
    in_specs = [
        pl.BlockSpec((TR_FF, D_FF), row),
        pl.BlockSpec((None, D_FF, D), lambda i: (layer, 0, 0), pipeline_mode=pl.Buffered(1)),
        pl.BlockSpec((TR_FF, D), row),
        _mod_spec(layer, _MOD_G2, TR_FF),
        pl.BlockSpec((1, D), lambda i: (0, 0)),
    ]
    if final:
        return pl.pallas_call(
            _ffn_final_kernel,
            out_shape=(jax.ShapeDtypeStruct((T_P, D), f32), jax.ShapeDtypeStruct((T_S, D), f32)),
            grid=(T // TR_FF,),
            in_specs=in_specs,
            out_specs=(pl.BlockSpec((TR_FF, D), lambda i: (jnp.minimum(i, n_p - 1), 0)),
                       pl.BlockSpec((TR_FF, D), lambda i: (jnp.maximum(i - n_p, 0), 0))),
            scratch_shapes=[pltpu.VMEM((TR_FF, D), f32)],
            compiler_params=_cparams(("arbitrary",)),
            name=f"ffn{layer}",
        )(act, wd_bf, x_mid, mods, gain.reshape(1, D))
    return pl.pallas_call(
        _ffn_kernel,
        out_shape=(jax.ShapeDtypeStruct((T, D), f32), jax.ShapeDtypeStruct((T, D), bf16)),
        grid=(T // TR_FF,),
        in_specs=in_specs + [_mod_spec(layer + 1, _MOD_SH1, TR_FF), _mod_spec(layer + 1, _MOD_SC1, TR_FF)],
        out_specs=(pl.BlockSpec((TR_FF, D), row), pl.BlockSpec((TR_FF, D), row)),
        compiler_params=_cparams(("arbitrary",)),
        name=f"ffn{layer}",
    )(act, wd_bf, x_mid, mods, gain.reshape(1, D), mods, mods)


def _to_rows(a):
    f = a.shape[-1]
    return a.reshape(N_SG, SB, DEC_SEQ, f).transpose(0, 2, 1, 3).reshape(T_S, f)


def _from_rows(a, steps=DEC_SEQ):
    f = a.shape[-1]
    return a.reshape(N_SG, steps, SB, f).transpose(0, 2, 1, 3).reshape(DEC_BATCH, steps, f)


def kernel(x_prompt, x_sample, c_prompt, c_sample, state_hgrn, state_ffn_conv, ada_w, ada_b, norm1_g, norm2_g,
           w_in, gmlp_v_g, gmlp_ws, gmlp_bs, w_branch_a, hgrn_lower_bounds, hgrn_norm_g, w_branch_b, w_out,
           w_up, conv_w, conv_b, w_down, final_norm_g):
    xs = (x_prompt.reshape(T_P, D), _to_rows(x_sample))
    c_pad = -(DEC_BATCH + BATCH) % SUBLANES
    c_all = jnp.concatenate([c_sample, c_prompt, jnp.zeros((c_pad, D), f32)], axis=0)
    mods = _ada(c_all, ada_w, ada_b)

    wout_bf = w_out.astype(bf16)
    wd_bf = w_down.astype(bf16)
    conv_st = state_ffn_conv.transpose(0, 2, 1, 3).reshape(DEPTH, CONV_W - 1, N_SG, SB, 2 * D_FF)

    h = _norm_mod(*xs, norm1_g[0], mods, 0)
    hs_all = None
    hp, cp, cs, vp, vs = [], [], [], [], []
    y_p = y_s = None
    for l in range(DEPTH):
        zu, zv, zqo, zf, zi, zg = _inproj(h, w_in, hgrn_lower_bounds, gmlp_v_g, l)
        a_gated = _gmlp(zu, zv, zg, gmlp_ws[l], gmlp_bs[l], w_branch_a, l)
        on_p, s_p = _hgrn_prompt(zqo, zf, zi, hgrn_norm_g, l)
        on_s, hs_all = _hgrn_sample(zqo, zf, zi, state_hgrn, hs_all, hgrn_norm_g, l)
        ymix = _branch_mix(on_p, on_s, w_branch_b, a_gated, zg, l)
        x_mid, h2 = _outproj(ymix, wout_bf, xs, norm2_g, mods, l)
        act, tail_a, tail_b = _up_act(h2, w_up, conv_w, conv_b, conv_st[l], l)
        if l == DEPTH - 1:
            y_p, y_s = _ffn(act, wd_bf, x_mid, mods, final_norm_g, l, True)
        else:
            x, h = _ffn(act, wd_bf, x_mid, mods, norm1_g[l + 1], l, False)
            xs = (x,)
        tail = jnp.concatenate([tail_a, tail_b], axis=1)
        hp.append(s_p)
        cp.append(tail[:SUBLANES * BATCH].reshape(BATCH, SUBLANES, 2 * D_FF)[:, SUBLANES - (CONV_W - 1):])
        cs.append(_from_rows(tail[TAIL_P:], CONV_W - 1))
        vp.append(jnp.stack([zv[(b + 1) * SEQ - CHUNK:(b + 1) * SEQ] for b in range(BATCH)]))
        vs.append(_from_rows(zv[T_P:]))

    y_prompt = y_p.reshape(BATCH, SEQ, D)
    y_sample = _from_rows(y_s)
    return (y_prompt, y_sample, jnp.stack(hp), hs_all, jnp.stack(cp), jnp.stack(cs), jnp.stack(vp), jnp.stack(vs))
```

```python
import functools

import numpy as np
import jax
import jax.numpy as jnp
from jax import lax
from jax.experimental import pallas as pl
from jax.experimental.pallas import tpu as pltpu

f32 = jnp.float32
bf16 = jnp.bfloat16

D = 2048
BATCH = 4
SEQ = 2048
DEPTH = 2
DEC_BATCH = 128
DEC_SEQ = 8
G_A = 8
DG_A = 128
D_A = G_A * DG_A
CHUNK = 128
H_B = 16
DK = 128
DV = 128
D_FF = 5632
CONV_W = 3
N_IN = 2 * D_A + 6 * D
EPS = 1e-6

T_P = BATCH * SEQ
T_S = DEC_BATCH * DEC_SEQ
T = T_P + T_S
SB = 64
N_SG = DEC_BATCH // SB
TR = DEC_SEQ * SB
N_RB = T // TR
N_PRB = T_P // TR
RB_PER_SEQ = SEQ // TR
TM = 1024
N_MB = T // TM
N_PMB = T_P // TM
MB_PER_SEQ = SEQ // TM
TN = 1024
TC_FF = 512
N_CFF = D_FF // TC_FF
TR_FF = 256
HG_C = 128
HG_NH = 8
HG_BLOCK_LEVELS = (64, 32, 16)
HG_MASK_LEVELS = (8, 4, 2, 1)
HS_UNROLL = 32
HS_NBUF = 3
SUBLANES = 8
LOG2E = 1.4426950408889634
TAIL_P = 128
TAIL_ROWS = TAIL_P + (CONV_W - 1) * DEC_BATCH
assert BATCH * SUBLANES <= TAIL_P

_U0, _V0, _Q0, _F0, _I0, _OG0, _GA0 = 0, 1, 2, 4, 6, 8, 10
_MOD_SH1, _MOD_SC1, _MOD_G1, _MOD_SH2, _MOD_SC2, _MOD_G2 = range(6)
N_MODJ = BATCH + N_SG


def _cparams(sem):
    return pltpu.CompilerParams(dimension_semantics=sem)


def _hgrn_level_table(c):
    t = np.arange(c)
    lev = np.full((c, c), -1, np.int32)
    lev[t, t] = 0
    for h in HG_MASK_LEVELS:
        upper = (t % (2 * h)) >= h
        same = (t[:, None] // (2 * h)) == (t[None, :] // (2 * h))
        lev[same & upper[:, None] & ~upper[None, :]] = h
    return lev


_LEV_NP = _hgrn_level_table(HG_C)


def _rms(x, eps=EPS):
    return x * lax.rsqrt(jnp.mean(x * x, axis=-1, keepdims=True) + eps)


def _dot(a, b):
    return jnp.dot(a, b, preferred_element_type=f32)


def _dot_nt(a, b):
    return lax.dot_general(a, b, (((1,), (1,)), ((), ())), preferred_element_type=f32)


def _dot_tn(a, b):
    return lax.dot_general(a, b, (((0,), (0,)), ((), ())), preferred_element_type=f32)


def _modj(i, rows):
    return jnp.where(i < T_P // rows, i // (SEQ // rows), BATCH + (i - T_P // rows) // (TR // rows))


def _mod_spec(layer, kind, rows=TR):
    return pl.BlockSpec((None, None, None, SB, D), lambda i, *_: (layer, kind, _modj(i, rows), 0, 0))


ADA_NCOL = 6 * D // TN
ADA_NBUF = 3


def _ada_kernel(c_ref, w_hbm, b_ref, o_ref, w_buf, w_sem):
    step = pl.program_id(0) * ADA_NCOL + pl.program_id(1)
    n_steps = DEPTH * ADA_NCOL

    def w_copy(t, slot):
        col = (t % ADA_NCOL) * TN
        col = col if isinstance(col, int) else pl.multiple_of(col, TN)
        return pltpu.make_async_copy(w_hbm.at[t // ADA_NCOL, :, pl.ds(col, TN)], w_buf.at[slot], w_sem.at[slot])

    @pl.when(step == 0)
    def _():
        for t in range(ADA_NBUF - 1):
            w_copy(t, t).start()

    ahead = step + (ADA_NBUF - 1)

    @pl.when(ahead < n_steps)
    def _():
        w_copy(ahead, ahead % ADA_NBUF).start()

    slot = step % ADA_NBUF
    c = c_ref[...]
    a = (c * jax.nn.sigmoid(c)).astype(bf16)
    w_copy(step, slot).wait()
    r = _dot(a, w_buf[slot].astype(bf16)) + b_ref[...]
    for g in range(N_SG):
        o_ref[BATCH + g] = r[g * SB:(g + 1) * SB]
    for j in range(BATCH):
        o_ref[j] = jnp.broadcast_to(r[DEC_BATCH + j:DEC_BATCH + j + 1], (SB, TN))


def _ada(c_all, ada_w, ada_b):
    nrow = c_all.shape[0]
    per_kind = D // TN
    return pl.pallas_call(
        _ada_kernel,
        out_shape=jax.ShapeDtypeStruct((DEPTH, 6, N_MODJ, SB, D), f32),
        grid=(DEPTH, ADA_NCOL),
        in_specs=[
            pl.BlockSpec((nrow, D), lambda l, n: (0, 0)),
            pl.BlockSpec(memory_space=pl.ANY),
            pl.BlockSpec((None, 1, TN), lambda l, n: (l, 0, n)),
        ],
        out_specs=pl.BlockSpec((None, None, N_MODJ, SB, TN), lambda l, n: (l, n // per_kind, 0, 0, n % per_kind)),
        scratch_shapes=[pltpu.VMEM((ADA_NBUF, D, TN), f32), pltpu.SemaphoreType.DMA((ADA_NBUF,))],
        compiler_params=_cparams(("arbitrary", "arbitrary")),
        name="ada_mod",
    )(c_all, ada_w, ada_b.reshape(DEPTH, 1, 6 * D))


def _x_specs(rows):
    n_p = T_P // rows
    return [pl.BlockSpec((rows, D), lambda i: (jnp.minimum(i, n_p - 1), 0)),
            pl.BlockSpec((rows, D), lambda i: (jnp.maximum(i - n_p, 0), 0))]


def _norm_mod_kernel(xp_ref, xs_ref, g_ref, sh_ref, sc_ref, o_ref):
    g = g_ref[...]
    sh = sh_ref[...]
    sc1 = 1.0 + sc_ref[...]
    is_prompt = pl.program_id(0) < N_PRB
    for s in range(TR // SB):
        rows = slice(s * SB, (s + 1) * SB)
        x = jnp.where(is_prompt, xp_ref[rows, :], xs_ref[rows, :])
        o_ref[rows, :] = (_rms(x) * g * sc1 + sh).astype(bf16)


def _norm_mod(xp, xs, gain, mods, layer):
    return pl.pallas_call(
        _norm_mod_kernel,
        out_shape=jax.ShapeDtypeStruct((T, D), bf16),
        grid=(N_RB,),
        in_specs=[
            *_x_specs(TR),
            pl.BlockSpec((1, D), lambda i: (0, 0)),
            _mod_spec(layer, _MOD_SH1),
            _mod_spec(layer, _MOD_SC1),
        ],
        out_specs=pl.BlockSpec((TR, D), lambda i: (i, 0)),
        compiler_params=_cparams(("arbitrary",)),
        name="norm_mod",
    )(xp, xs, gain.reshape(1, D), mods, mods)


def _proj_kernel(h_ref, w_ref, *rest, epilogue):
    *extra, o_ref, wbf_ref = rest

    @pl.when(pl.program_id(1) == 0)
    def _():
        wbf_ref[...] = w_ref[...].astype(bf16)

    epilogue(_dot(h_ref[...], wbf_ref[...]), o_ref, *extra)


def _epi_gelu(acc, o_ref):
    o_ref[...] = jax.nn.gelu(acc).astype(o_ref.dtype)


def _epi_v(acc, o_ref, vg_ref):
    v = jax.nn.gelu(acc)
    for g in range(G_A):
        cols = slice(g * DG_A, (g + 1) * DG_A)
        o_ref[:, cols] = _rms(v[:, cols]) * vg_ref[:, cols]


def _epi_silu(acc, o_ref, scale_ref):
    o_ref[...] = (acc * jax.nn.sigmoid(acc) * scale_ref[pl.program_id(0)]).astype(o_ref.dtype)


def _epi_logf(acc, o_ref, lb_ref, *, layer):
    p = jax.nn.softmax(lb_ref[...], axis=0)
    lb = jnp.zeros((1, TN), f32)
    for j in range(1, layer + 1):
        lb = lb + p[j:j + 1]
    a = jnp.log(lb)
    b = jnp.log1p(-lb) + jnp.minimum(acc, 0.0) - jnp.log(1.0 + jnp.exp(-jnp.abs(acc)))
    o_ref[...] = jnp.maximum(a, b) + jnp.log(1.0 + jnp.exp(-jnp.abs(a - b)))


def _epi_id(acc, o_ref):
    o_ref[...] = acc.astype(o_ref.dtype)


def _epi_sigmoid(acc, o_ref):
    o_ref[...] = jax.nn.sigmoid(acc).astype(o_ref.dtype)


def _proj(name, h, w_in, layer, col_of, n_tiles, out_dtype, epilogue, extra=(), extra_specs=()):
    return pl.pallas_call(
        functools.partial(_proj_kernel, epilogue=epilogue),
        out_shape=jax.ShapeDtypeStruct((T, n_tiles * TN), out_dtype),
        grid=(n_tiles, N_MB),
        in_specs=[
            pl.BlockSpec((TM, D), lambda n, m: (m, 0)),
            pl.BlockSpec((None, D, TN), lambda n, m: (layer, 0, col_of(n))),
            *extra_specs,
        ],
        out_specs=pl.BlockSpec((TM, TN), lambda n, m: (m, n)),
        scratch_shapes=[pltpu.VMEM((D, TN), bf16)],
        compiler_params=_cparams(("arbitrary", "arbitrary")),
        name=f"{name}{layer}",
    )(h, w_in, *extra)


def _inproj(h, w_in, lower_bounds, v_gain, layer):
    smem = pl.BlockSpec(memory_space=pltpu.SMEM)
    nd = D // TN
    zu = _proj("in_u", h, w_in, layer, lambda n: _U0 + n, D_A // TN, bf16, _epi_gelu)
    zv = _proj("in_v", h, w_in, layer, lambda n: _V0 + n, D_A // TN, f32, _epi_v,
               (v_gain.reshape(DEPTH, 1, D_A),), (pl.BlockSpec((None, 1, D_A), lambda n, m: (layer, 0, 0)),))
    scales = jnp.asarray([DK ** -0.5] * nd + [1.0] * nd, f32)
    zqo = _proj("in_qo", h, w_in, layer, lambda n: jnp.where(n < nd, _Q0 + n, _OG0 + n - nd), 2 * nd, bf16, _epi_silu,
                (scales,), (smem,))
    zf = _proj("in_f", h, w_in, layer, lambda n: _F0 + n, nd, f32, functools.partial(_epi_logf, layer=layer),
               (lower_bounds,), (pl.BlockSpec((DEPTH, TN), lambda n, m: (0, n)),))
    zi = _proj("in_i", h, w_in, layer, lambda n: _I0 + n, nd, bf16, _epi_id)
    zg = _proj("in_g", h, w_in, layer, lambda n: _GA0 + n, 2 * nd, bf16, _epi_sigmoid)
    return zu, zv, zqo, zf, zi, zg


def _gmlp_kernel(u_ref, v_ref, ga_ref, ws_ref, bias_ref, w8_ref, b8_ref, wba_ref, o_ref, wm_ref, ya_ref, wbf_ref):
    i = pl.program_id(0)

    @pl.when(i == 0)
    def _():
        wbf_ref[...] = wba_ref[...].astype(bf16)
        r = lax.broadcasted_iota(jnp.int32, (CHUNK, CHUNK), 0)
        c = lax.broadcasted_iota(jnp.int32, (CHUNK, CHUNK), 1)
        for g in range(G_A):
            wm_ref[g] = jnp.where(r >= c, ws_ref[g], 0.0).astype(bf16)

    @pl.when(i < N_PRB)
    def _():
        for ch in range(TR // CHUNK):
            rows = slice(ch * CHUNK, (ch + 1) * CHUNK)
            for g in range(G_A):
                cols = slice(g * DG_A, (g + 1) * DG_A)
                s = _dot(wm_ref[g], v_ref[rows, cols].astype(bf16)) + bias_ref[:, cols]
                ya_ref[rows, cols] = (u_ref[rows, cols].astype(f32) * s).astype(bf16)

    @pl.when(i >= N_PRB)
    def _():
        for g in range(G_A):
            cols = slice(g * DG_A, (g + 1) * DG_A)
            vs = [v_ref[s * SB:(s + 1) * SB, cols] for s in range(DEC_SEQ)]
            for t in range(DEC_SEQ):
                w0 = (g * DEC_SEQ + t) * DEC_SEQ
                acc = vs[0] * w8_ref[w0]
                for s in range(1, t + 1):
                    acc = acc + vs[s] * w8_ref[w0 + s]
                acc = acc + b8_ref[g * DEC_SEQ + t]
                ya_ref[t * SB:(t + 1) * SB, cols] = (u_ref[t * SB:(t + 1) * SB, cols].astype(f32) * acc).astype(bf16)

    o_ref[...] = (ga_ref[...].astype(f32) * _dot(ya_ref[...], wbf_ref[...])).astype(bf16)


def _gmlp(zu, zv, zg, ws, bs, w_ba, layer):
    bias_full = jnp.repeat(bs.T, DG_A, axis=1)
    w8 = ws[:, :DEC_SEQ, :DEC_SEQ].reshape(G_A * DEC_SEQ * DEC_SEQ)
    b8 = bs[:, :DEC_SEQ].reshape(G_A * DEC_SEQ)
    return pl.pallas_call(
        _gmlp_kernel,
        out_shape=jax.ShapeDtypeStruct((T, D), bf16),
        grid=(N_RB,),
        in_specs=[
            pl.BlockSpec((TR, D_A), lambda i: (i, 0)),
            pl.BlockSpec((TR, D_A), lambda i: (i, 0)),
            pl.BlockSpec((TR, D), lambda i: (i, 0)),
            pl.BlockSpec((G_A, CHUNK, CHUNK), lambda i: (0, 0, 0)),
            pl.BlockSpec((CHUNK, D_A), lambda i: (0, 0)),
            pl.BlockSpec(memory_space=pltpu.SMEM),
            pl.BlockSpec(memory_space=pltpu.SMEM),
            pl.BlockSpec((None, D_A, D), lambda i: (layer, 0, 0), pipeline_mode=pl.Buffered(1)),
        ],
        out_specs=pl.BlockSpec((TR, D), lambda i: (i, 0)),
        scratch_shapes=[pltpu.VMEM((G_A, CHUNK, CHUNK), bf16), pltpu.VMEM((TR, D_A), bf16),
                        pltpu.VMEM((D_A, D), bf16)],
        compiler_params=_cparams(("arbitrary",)),
        name=f"gmlp{layer}",
    )(zu, zv, zg, ws, bias_full, w8, b8, w_ba)


def _bcast_row(x8, r):
    return jnp.broadcast_to(x8[r:r + 1, :], x8.shape)


def _cat_rows(tiles):
    return tiles[0] if len(tiles) == 1 else jnp.concatenate(tiles, axis=0)


def _hgrn_chunk(q, g, v, sts, lev, r8, nh):
    c = HG_C
    nv = c // 8
    w = nh * DK
    hs = [slice(i * DK, (i + 1) * DK) for i in range(nh)]
    gl = g * LOG2E
    x_t = []
    for j in range(nv):
        x = gl[8 * j:8 * j + 8, :]
        for sh in (1, 2, 4):
            x = x + jnp.where(r8 >= sh, pltpu.roll(x, sh, 0), 0.0)
        x_t.append(x)
    last = [_bcast_row(x, 7) for x in x_t]
    sh = 1
    while sh < nv:
        last = [last[j] + last[j - sh] if j >= sh else last[j] for j in range(nv)]
        sh *= 2
    a_t = [x_t[0]] + [x_t[j] + last[j - 1] for j in range(1, nv)]
    f = jnp.exp2(gl)
    kk = 1.0 - f
    q_t = [q[8 * j:8 * j + 8, :] for j in range(nv)]
    k_t = [kk[8 * j:8 * j + 8, :] for j in range(nv)]
    a = _cat_rows(a_t)
    odd = (lax.broadcasted_iota(jnp.int32, (c, w), 0) & 1) == 1

    qb = q.astype(bf16)
    kb0 = kk.astype(bf16)
    sc = [jnp.where(lev == 0, _dot_nt(qb[:, hh], kb0[:, hh]), 0.0) for hh in hs]
    for h in HG_MASK_LEVELS:
        if h == 8:
            e = _cat_rows([(a_t[j] - last[j - 1]) if j % 2 else (last[j] - a_t[j]) for j in range(nv)])
        elif h == 4:
            d = _cat_rows([t - _bcast_row(t, 3) for t in a_t])
            e = jnp.minimum(d, -d)
        elif h == 2:
            d = _cat_rows([t - jnp.where(r8 < 4, _bcast_row(t, 1), _bcast_row(t, 5)) for t in a_t])
            e = jnp.minimum(d, -d)
        if h >= 2:
            e = jnp.exp2(e)
        else:
            e = jnp.where(odd, f, 1.0)
        qe = (q * e).astype(bf16)
        ke = (kk * e).astype(bf16)
        sc = [jnp.where(lev == h, _dot_nt(qe[:, hh], ke[:, hh]), sc[i]) for i, hh in enumerate(hs)]
    sc_t = [[s_[8 * j:8 * j + 8, :] for j in range(nv)] for s_ in sc]
    zero8 = jnp.zeros((8, w), f32)
    for h in HG_BLOCK_LEVELS:
        ht = h // 8
        for blk in range(c // (2 * h)):
            lo = range(blk * 2 * ht, blk * 2 * ht + ht)
            up = range(blk * 2 * ht + ht, (blk + 1) * 2 * ht)
            ref = last[blk * 2 * ht + ht - 1]
            qt = _cat_rows([q_t[j] * jnp.exp2(a_t[j] - ref) for j in up]).astype(bf16)
            kz = _cat_rows([k_t[j] * jnp.exp2(ref - a_t[j]) if j in lo else zero8 for j in range(nv)]).astype(bf16)
            for i, hh in enumerate(hs):
                s_ = _dot_nt(qt[:, hh], kz[:, hh])
                for r, j in enumerate(up):
                    sc_t[i][j] = sc_t[i][j] + s_[8 * r:8 * r + 8, :]
    qa = (q * jnp.exp2(a)).astype(bf16)
    a_last = last[-1]
    kb = _cat_rows([k_t[j] * jnp.exp2(a_last - a_t[j]) for j in range(nv)]).astype(bf16)
    fl = jnp.exp2(a_last[0:1, :])
    o, st_new = [], []
    for i, hh in enumerate(hs):
        o.append(_dot_nt(qa[:, hh], sts[i].astype(bf16)) + _dot(_cat_rows(sc_t[i]).astype(bf16), v[:, hh]))
        st_new.append(fl[:, hh] * sts[i] + _dot_tn(v[:, hh], kb[:, hh]))
    return o, st_new


def _hgrn_prompt_kernel(q_ref, g_ref, v_ref, og_ref, lev_ref, gain_ref, on_ref, so_ref, st_ref):
    r8 = lax.broadcasted_iota(jnp.int32, (8, HG_NH * DK), 0)
    for hh in range(HG_NH):
        st_ref[hh] = jnp.zeros((DV, DK), f32)

    def chunk(c, carry):
        r = pl.ds(pl.multiple_of(c * HG_C, HG_C), HG_C)
        o, st_new = _hgrn_chunk(q_ref[r, :].astype(f32), g_ref[r, :], v_ref[r, :],
                                [st_ref[hh] for hh in range(HG_NH)], lev_ref[...], r8, HG_NH)
        for hh in range(HG_NH):
            cols = slice(hh * DV, (hh + 1) * DV)
            on_ref[r, cols] = (_rms(o[hh]) * gain_ref[:, cols] * og_ref[r, cols].astype(f32)).astype(bf16)
            st_ref[hh] = st_new[hh]
        return carry

    lax.fori_loop(0, SEQ // HG_C, chunk, 0)
    for hh in range(HG_NH):
        so_ref[hh] = st_ref[hh].T


def _hgrn_prompt(zqo, zf, zi, gain, layer):
    ng = H_B // HG_NH
    spec = lambda off: pl.BlockSpec((SEQ, HG_NH * DK), lambda b, h: (b, off + h))
    return pl.pallas_call(
        _hgrn_prompt_kernel,
        out_shape=(jax.ShapeDtypeStruct((T_P, D), bf16), jax.ShapeDtypeStruct((BATCH, H_B, DK, DV), f32)),
        grid=(BATCH, ng),
        in_specs=[
            spec(0), spec(0), spec(0), spec(ng),
            pl.BlockSpec((HG_C, HG_C), lambda b, h: (0, 0)),
            pl.BlockSpec((None, 1, HG_NH * DV), lambda b, h: (layer, 0, h)),
        ],
        out_specs=(pl.BlockSpec((SEQ, HG_NH * DV), lambda b, h: (b, h)),
                   pl.BlockSpec((None, HG_NH, DK, DV), lambda b, h: (b, h, 0, 0))),
        scratch_shapes=[pltpu.VMEM((HG_NH, DV, DK), f32)],
        compiler_params=_cparams(("arbitrary", "arbitrary")),
        name=f"hgrn_prompt{layer}",
    )(zqo, zf, zi, zqo, jnp.asarray(_LEV_NP), gain.reshape(DEPTH, 1, D))


def _hgrn_sample_kernel(*refs, layer):
    q_ref, g_ref, v_ref, og_ref, s_hbm, gain_ref = refs[:6]
    on_ref, so_ref, qa_scr, kb_scr, v_scr, fs_scr, o_scr, s_buf, s_sem = refs[-9:]
    slab = lambda t: slice(t * SB, (t + 1) * SB)
    step = pl.program_id(0) * H_B + pl.program_id(1)
    n_steps = N_SG * H_B

    def state_copy(t, slot):
        src = s_hbm.at[layer, pl.ds((t // H_B) * SB, SB), t % H_B]
        return pltpu.make_async_copy(src, s_buf.at[slot], s_sem.at[slot])

    @pl.when(step == 0)
    def _():
        for t in range(HS_NBUF - 1):
            state_copy(t, t).start()

    ahead = step + (HS_NBUF - 1)

    @pl.when(ahead < n_steps)
    def _():
        state_copy(ahead, ahead % HS_NBUF).start()

    slot = step % HS_NBUF
    a, q, kk, v = [], [], [], []
    for t in range(DEC_SEQ):
        gl = g_ref[slab(t), :] * LOG2E
        a.append(gl if t == 0 else a[t - 1] + gl)
        kk.append(1.0 - jnp.exp2(gl))
        q.append(q_ref[slab(t), :].astype(f32))
        v.append(v_ref[slab(t), :].astype(f32))
    a_last = a[DEC_SEQ - 1]
    for t in range(DEC_SEQ):
        qa_scr[slab(t), :] = q[t] * jnp.exp2(a[t])
        kb_scr[slab(t), :] = kk[t] * jnp.exp2(a_last - a[t])
        v_scr[slab(t), :] = v[t]
        o = jnp.sum(q[t] * kk[t], axis=-1, keepdims=True) * v[t]
        for s in range(t):
            o = o + jnp.sum(q[t] * kk[s] * jnp.exp2(a[t] - a[s]), axis=-1, keepdims=True) * v[s]
        o_scr[slab(t), :] = o
    fl = jnp.exp2(a_last)
    f1 = fl.astype(bf16).astype(f32)
    r1 = fl - f1
    f2 = r1.astype(bf16).astype(f32)
    fs_scr[slab(0), :] = f1
    fs_scr[slab(1), :] = f2
    fs_scr[slab(2), :] = r1 - f2
    fs_scr[3 * SB:, :] = jnp.zeros((TR - 3 * SB, DK), f32)

    ones3 = (lax.broadcasted_iota(jnp.int32, (DEC_SEQ, DV), 0) < 3).astype(bf16)
    zero8 = jnp.zeros((DEC_SEQ, DV), bf16)

    state_copy(step, slot).wait()

    def per_batch(b, carry):
        rows = pl.ds(b, DEC_SEQ, stride=SB)
        s0 = s_buf[slot, b]
        o_scr[rows, :] = o_scr[rows, :] + _dot(qa_scr[rows, :].astype(bf16), s0.astype(bf16))
        lhs = jnp.concatenate([kb_scr[rows, :], fs_scr[rows, :]], axis=0).astype(bf16)
        v8 = v_scr[rows, :].astype(bf16)
        rhs = jnp.concatenate([jnp.concatenate([v8, zero8], axis=1), jnp.concatenate([zero8, ones3], axis=1)], axis=0)
        r = _dot_tn(lhs, rhs)
        so_ref[b] = r[:, DV:] * s0 + r[:, :DV]
        return carry

    lax.fori_loop(0, SB, per_batch, 0, unroll=HS_UNROLL)
    gain = gain_ref[...]
    for t in range(DEC_SEQ):
        on_ref[slab(t), :] = (_rms(o_scr[slab(t), :]) * gain * og_ref[slab(t), :].astype(f32)).astype(bf16)


def _hgrn_sample(zqo, zf, zi, state_hgrn, hs_prev, gain, layer):
    rb0 = T_P // TR
    spec = lambda off: pl.BlockSpec((TR, DK), lambda g, h: (rb0 + g, off + h))
    sspec = pl.BlockSpec((None, SB, None, DK, DV), lambda g, h: (layer, g, h, 0, 0))
    ins = [zqo, zf, zi, zqo, state_hgrn, gain.reshape(DEPTH, 1, D)]
    in_specs = [spec(0), spec(0), spec(0), spec(H_B), pl.BlockSpec(memory_space=pl.ANY),
                pl.BlockSpec((None, 1, DV), lambda g, h: (layer, 0, h))]
    aliases = {}
    if hs_prev is not None:
        ins.append(hs_prev)
        in_specs.append(pl.BlockSpec(memory_space=pl.ANY))
        aliases[6] = 1
    return pl.pallas_call(
        functools.partial(_hgrn_sample_kernel, layer=layer),
        out_shape=(jax.ShapeDtypeStruct((T_S, D), bf16), jax.ShapeDtypeStruct(state_hgrn.shape, f32)),
        grid=(N_SG, H_B),
        in_specs=in_specs,
        out_specs=(pl.BlockSpec((TR, DV), lambda g, h: (g, h)), sspec),
        scratch_shapes=[pltpu.VMEM((TR, DV), f32)] * 5 + [pltpu.VMEM((HS_NBUF, SB, DK, DV), f32),
                                                          pltpu.SemaphoreType.DMA((HS_NBUF,))],
        input_output_aliases=aliases,
        compiler_params=_cparams(("arbitrary", "arbitrary")),
        name=f"hgrn_sample{layer}",
    )(*ins)


def _mix_kernel(onp_ref, ons_ref, w_ref, a_ref, gb_ref, o_ref, wbf_ref):
    m = pl.program_id(1)

    @pl.when(m == 0)
    def _():
        wbf_ref[...] = w_ref[...].astype(bf16)

    on = jnp.where(m < N_PMB, onp_ref[...], ons_ref[...])
    o_ref[...] = (a_ref[...].astype(f32) + gb_ref[...].astype(f32) * _dot(on, wbf_ref[...])).astype(bf16)


def _branch_mix(on_p, on_s, w_bb, a_gated, zg, layer):
    return pl.pallas_call(
        _mix_kernel,
        out_shape=jax.ShapeDtypeStruct((T, D), bf16),
        grid=(D // TN, N_MB),
        in_specs=[
            pl.BlockSpec((TM, D), lambda n, m: (jnp.minimum(m, N_PMB - 1), 0)),
            pl.BlockSpec((TM, D), lambda n, m: (jnp.maximum(m - N_PMB, 0), 0), pipeline_mode=pl.Buffered(1)),
            pl.BlockSpec((None, D, TN), lambda n, m: (layer, 0, n)),
            pl.BlockSpec((TM, TN), lambda n, m: (m, n)),
            pl.BlockSpec((TM, TN), lambda n, m: (m, D // TN + n)),
        ],
        out_specs=pl.BlockSpec((TM, TN), lambda n, m: (m, n)),
        scratch_shapes=[pltpu.VMEM((D, TN), bf16)],
        compiler_params=_cparams(("arbitrary", "arbitrary")),
        name=f"branch_mix{layer}",
    )(on_p, on_s, w_bb, a_gated, zg)


def _outproj_kernel(y_ref, w_ref, *rest, split_x):
    *x_refs, n2_ref, g1_ref, sh_ref, sc_ref, xo_ref, h_ref = rest
    xo_ref[...] = _dot(y_ref[...], w_ref[...])
    n2 = n2_ref[...]
    g1 = g1_ref[...]
    sh = sh_ref[...]
    sc1 = 1.0 + sc_ref[...]
    is_prompt = pl.program_id(0) < N_PRB
    for s in range(TR // SB):
        rows = slice(s * SB, (s + 1) * SB)
        x = jnp.where(is_prompt, x_refs[0][rows, :], x_refs[1][rows, :]) if split_x else x_refs[0][rows, :]
        xm = x + g1 * xo_ref[rows, :]
        xo_ref[rows, :] = xm
        h_ref[rows, :] = (_rms(xm) * n2 * sc1 + sh).astype(bf16)


def _outproj(y, w_out_bf, xs, n2, mods, layer):
    split_x = len(xs) == 2
    return pl.pallas_call(
        functools.partial(_outproj_kernel, split_x=split_x),
        out_shape=(jax.ShapeDtypeStruct((T, D), f32), jax.ShapeDtypeStruct((T, D), bf16)),
        grid=(N_RB,),
        in_specs=[
            pl.BlockSpec((TR, D), lambda i: (i, 0)),
            pl.BlockSpec((None, D, D), lambda i: (layer, 0, 0), pipeline_mode=pl.Buffered(1)),
            *(_x_specs(TR) if split_x else [pl.BlockSpec((TR, D), lambda i: (i, 0))]),
            pl.BlockSpec((None, 1, D), lambda i: (layer, 0, 0)),
            _mod_spec(layer, _MOD_G1), _mod_spec(layer, _MOD_SH2), _mod_spec(layer, _MOD_SC2),
        ],
        out_specs=(pl.BlockSpec((TR, D), lambda i: (i, 0)), pl.BlockSpec((TR, D), lambda i: (i, 0))),
        compiler_params=_cparams(("arbitrary",)),
        name=f"outproj{layer}",
    )(y, w_out_bf, *xs, n2.reshape(DEPTH, 1, D), mods, mods, mods)


def _conv3(cur, m1, m2, cw_ref, cb_ref):
    return cb_ref[...] + cw_ref[0:1, :] * m2 + cw_ref[1:2, :] * m1 + cw_ref[2:3, :] * cur


def _up_kernel(h_ref, wa_ref, wb_ref, cwa_ref, cwb_ref, cba_ref, cbb_ref, sa_ref, sb_ref,
               act_ref, ta_ref, tb_ref, wabf_ref, wbbf_ref, haloa_ref, halob_ref):
    m = pl.program_id(1)

    @pl.when(m == 0)
    def _():
        wabf_ref[...] = wa_ref[...].astype(bf16)
        wbbf_ref[...] = wb_ref[...].astype(bf16)
        ta_ref[...] = jnp.zeros_like(ta_ref)
        tb_ref[...] = jnp.zeros_like(tb_ref)
        haloa_ref[...] = jnp.zeros_like(haloa_ref)
        halob_ref[...] = jnp.zeros_like(halob_ref)

    h = h_ref[...]
    acc_a = _dot(h, wabf_ref[...])
    acc_b = _dot(h, wbbf_ref[...])
    seq_start = (m % MB_PER_SEQ) == 0
    r8 = lax.broadcasted_iota(jnp.int32, (8, TC_FF), 0)

    def conv(cur, halo_ref, cw_ref, cb_ref):
        body = _conv3(cur, pltpu.roll(cur, 1, 0), pltpu.roll(cur, 2, 0), cw_ref, cb_ref)
        halo = jnp.where(seq_start, 0.0, halo_ref[...])
        cur8 = cur[0:8, :]
        m1 = jnp.where(r8 < 1, pltpu.roll(halo, 1, 0), pltpu.roll(cur8, 1, 0))
        m2 = jnp.where(r8 < 2, pltpu.roll(halo, 2, 0), pltpu.roll(cur8, 2, 0))
        return body, _conv3(cur8, m1, m2, cw_ref, cb_ref)

    ca, ca8 = conv(acc_a, haloa_ref, cwa_ref, cba_ref)
    cb, cb8 = conv(acc_b, halob_ref, cwb_ref, cbb_ref)
    act_ref[...] = (jax.nn.gelu(ca) * cb).astype(bf16)
    act_ref[0:8, :] = (jax.nn.gelu(ca8) * cb8).astype(bf16)
    ta8 = acc_a[TM - 8:TM, :]
    tb8 = acc_b[TM - 8:TM, :]
    haloa_ref[...] = ta8
    halob_ref[...] = tb8

    @pl.when((m < N_PMB) & (m % MB_PER_SEQ == MB_PER_SEQ - 1))
    def _():
        r = pl.ds(pl.multiple_of((m // MB_PER_SEQ) * SUBLANES, SUBLANES), SUBLANES)
        ta_ref[r, :] = ta8
        tb_ref[r, :] = tb8

    @pl.when(m == N_PMB)
    def _():
        def conv_s(acc, st_ref, cw_ref, cb_ref, g, t):
            slab = lambda j: acc[(g * DEC_SEQ + j) * SB:(g * DEC_SEQ + j + 1) * SB, :]
            m1 = slab(t - 1) if t >= 1 else st_ref[1, g]
            m2 = slab(t - 2) if t >= 2 else st_ref[t, g]
            return _conv3(slab(t), m1, m2, cw_ref, cb_ref)

        for g in range(N_SG):
            for t in range(DEC_SEQ):
                ca_ = conv_s(acc_a, sa_ref, cwa_ref, cba_ref, g, t)
                cb_ = conv_s(acc_b, sb_ref, cwb_ref, cbb_ref, g, t)
                r0 = (g * DEC_SEQ + t) * SB
                act_ref[r0:r0 + SB, :] = (jax.nn.gelu(ca_) * cb_).astype(bf16)
            for j in range(CONV_W - 1):
                r0 = (g * DEC_SEQ + DEC_SEQ - (CONV_W - 1) + j) * SB
                d0 = TAIL_P + (g * (CONV_W - 1) + j) * SB
                ta_ref[d0:d0 + SB, :] = acc_a[r0:r0 + SB, :]
                tb_ref[d0:d0 + SB, :] = acc_b[r0:r0 + SB, :]


def _up_act(h, w_up, conv_w, conv_b, conv_state, layer):
    nb = N_CFF
    cws = lambda off: pl.BlockSpec((None, CONV_W, TC_FF), lambda n, m: (layer, 0, off + n))
    cbs = lambda off: pl.BlockSpec((None, 1, TC_FF), lambda n, m: (layer, 0, off + n))
    sts = lambda off: pl.BlockSpec((CONV_W - 1, N_SG, SB, TC_FF), lambda n, m: (0, 0, 0, off + n))
    ws = lambda off: pl.BlockSpec((None, D, TC_FF), lambda n, m: (layer, 0, off + n))
    cb3 = conv_b.reshape(DEPTH, 1, 2 * D_FF)
    return pl.pallas_call(
        _up_kernel,
        out_shape=(jax.ShapeDtypeStruct((T, D_FF), bf16), jax.ShapeDtypeStruct((TAIL_ROWS, D_FF), f32),
                   jax.ShapeDtypeStruct((TAIL_ROWS, D_FF), f32)),
        grid=(nb, N_MB),
        in_specs=[pl.BlockSpec((TM, D), lambda n, m: (m, 0)), ws(0), ws(nb), cws(0), cws(nb), cbs(0), cbs(nb),
                  sts(0), sts(nb)],
        out_specs=(pl.BlockSpec((TM, TC_FF), lambda n, m: (m, n)),
                   pl.BlockSpec((TAIL_ROWS, TC_FF), lambda n, m: (0, n)),
                   pl.BlockSpec((TAIL_ROWS, TC_FF), lambda n, m: (0, n))),
        scratch_shapes=[pltpu.VMEM((D, TC_FF), bf16), pltpu.VMEM((D, TC_FF), bf16),
                        pltpu.VMEM((8, TC_FF), f32), pltpu.VMEM((8, TC_FF), f32)],
        compiler_params=_cparams(("arbitrary", "arbitrary")),
        name=f"up_act{layer}",
    )(h, w_up, w_up, conv_w, conv_w, cb3, cb3, conv_state, conv_state)


def _ffn_kernel(act_ref, wd_ref, xm_ref, g2_ref, gain_ref, sh_ref, sc_ref, xo_ref, ho_ref):
    xo_ref[...] = _dot(act_ref[...], wd_ref[...])
    g2 = g2_ref[...]
    gain = gain_ref[...]
    for s in range(TR_FF // SB):
        rows = slice(s * SB, (s + 1) * SB)
        xn = xm_ref[rows, :] + g2 * xo_ref[rows, :]
        xo_ref[rows, :] = xn
        ho_ref[rows, :] = (_rms(xn) * gain * (1.0 + sc_ref[...]) + sh_ref[...]).astype(bf16)


def _ffn_final_kernel(act_ref, wd_ref, xm_ref, g2_ref, gain_ref, yp_ref, ys_ref, y_scr):
    y_scr[...] = _dot(act_ref[...], wd_ref[...])
    g2 = g2_ref[...]
    gain = gain_ref[...]
    for s in range(TR_FF // SB):
        rows = slice(s * SB, (s + 1) * SB)
        y_scr[rows, :] = _rms(xm_ref[rows, :] + g2 * y_scr[rows, :]) * gain
    is_prompt = pl.program_id(0) < T_P // TR_FF

    @pl.when(is_prompt)
    def _():
        yp_ref[...] = y_scr[...]

    @pl.when(jnp.logical_not(is_prompt))
    def _():
        ys_ref[...] = y_scr[...]


def _ffn(act, wd_bf, x_mid, mods, gain, layer, final):
    row = lambda i: (i, 0)
    n_p = T_P // TR_FF
    in_specs = [
        pl.BlockSpec((TR_FF, D_FF), row),
        pl.BlockSpec((None, D_FF, D), lambda i: (layer, 0, 0), pipeline_mode=pl.Buffered(1)),
        pl.BlockSpec((TR_FF, D), row),
        _mod_spec(layer, _MOD_G2, TR_FF),
        pl.BlockSpec((1, D), lambda i: (0, 0)),
    ]
    if final:
        return pl.pallas_call(
            _ffn_final_kernel,
            out_shape=(jax.ShapeDtypeStruct((T_P, D), f32), jax.ShapeDtypeStruct((T_S, D), f32)),
            grid=(T // TR_FF,),
            in_specs=in_specs,
            out_specs=(pl.BlockSpec((TR_FF, D), lambda i: (jnp.minimum(i, n_p - 1), 0)),
                       pl.BlockSpec((TR_FF, D), lambda i: (jnp.maximum(i - n_p, 0), 0))),
            scratch_shapes=[pltpu.VMEM((TR_FF, D), f32)],
            compiler_params=_cparams(("arbitrary",)),
            name=f"ffn{layer}",
        )(act, wd_bf, x_mid, mods, gain.reshape(1, D))
    return pl.pallas_call(
        _ffn_kernel,
        out_shape=(jax.ShapeDtypeStruct((T, D), f32), jax.ShapeDtypeStruct((T, D), bf16)),
        grid=(T // TR_FF,),
        in_specs=in_specs + [_mod_spec(layer + 1, _MOD_SH1, TR_FF), _mod_spec(layer + 1, _MOD_SC1, TR_FF)],
        out_specs=(pl.BlockSpec((TR_FF, D), row), pl.BlockSpec((TR_FF, D), row)),
        compiler_params=_cparams(("arbitrary",)),
        name=f"ffn{layer}",
    )(act, wd_bf, x_mid, mods, gain.reshape(1, D), mods, mods)


def _to_rows(a):
    f = a.shape[-1]
    return a.reshape(N_SG, SB, DEC_SEQ, f).transpose(0, 2, 1, 3).reshape(T_S, f)


def _from_rows(a, steps=DEC_SEQ):
    f = a.shape[-1]
    return a.reshape(N_SG, steps, SB, f).transpose(0, 2, 1, 3).reshape(DEC_BATCH, steps, f)


def kernel(x_prompt, x_sample, c_prompt, c_sample, state_hgrn, state_ffn_conv, ada_w, ada_b, norm1_g, norm2_g,
           w_in, gmlp_v_g, gmlp_ws, gmlp_bs, w_branch_a, hgrn_lower_bounds, hgrn_norm_g, w_branch_b, w_out,
           w_up, conv_w, conv_b, w_down, final_norm_g):
    xs = (x_prompt.reshape(T_P, D), _to_rows(x_sample))
    c_pad = -(DEC_BATCH + BATCH) % SUBLANES
    c_all = jnp.concatenate([c_sample, c_prompt, jnp.zeros((c_pad, D), f32)], axis=0)
    mods = _ada(c_all, ada_w, ada_b)

    wout_bf = w_out.astype(bf16)
    wd_bf = w_down.astype(bf16)
    conv_st = state_ffn_conv.transpose(0, 2, 1, 3).reshape(DEPTH, CONV_W - 1, N_SG, SB, 2 * D_FF)

    h = _norm_mod(*xs, norm1_g[0], mods, 0)
    hs_all = None
    hp, cp, cs, vp, vs = [], [], [], [], []
    y_p = y_s = None
    for l in range(DEPTH):
        zu, zv, zqo, zf, zi, zg = _inproj(h, w_in, hgrn_lower_bounds, gmlp_v_g, l)
        a_gated = _gmlp(zu, zv, zg, gmlp_ws[l], gmlp_bs[l], w_branch_a, l)
        on_p, s_p = _hgrn_prompt(zqo, zf, zi, hgrn_norm_g, l)
        on_s, hs_all = _hgrn_sample(zqo, zf, zi, state_hgrn, hs_all, hgrn_norm_g, l)
        ymix = _branch_mix(on_p, on_s, w_branch_b, a_gated, zg, l)
        x_mid, h2 = _outproj(ymix, wout_bf, xs, norm2_g, mods, l)
        act, tail_a, tail_b = _up_act(h2, w_up, conv_w, conv_b, conv_st[l], l)
        if l == DEPTH - 1:
            y_p, y_s = _ffn(act, wd_bf, x_mid, mods, final_norm_g, l, True)
        else:
            x, h = _ffn(act, wd_bf, x_mid, mods, norm1_g[l + 1], l, False)
            xs = (x,)
        tail = jnp.concatenate([tail_a, tail_b], axis=1)
        hp.append(s_p)
        cp.append(tail[:SUBLANES * BATCH].reshape(BATCH, SUBLANES, 2 * D_FF)[:, SUBLANES - (CONV_W - 1):])
        cs.append(_from_rows(tail[TAIL_P:], CONV_W - 1))
        vp.append(jnp.stack([zv[(b + 1) * SEQ - CHUNK:(b + 1) * SEQ] for b in range(BATCH)]))
        vs.append(_from_rows(zv[T_P:]))

    y_prompt = y_p.reshape(BATCH, SEQ, D)
    y_sample = _from_rows(y_s)
    return (y_prompt, y_sample, jnp.stack(hp), hs_all, jnp.stack(cp), jnp.stack(cs), jnp.stack(vp), jnp.stack(vs))
```
